```python
import math
import jax, jax.numpy as jnp
from jax import lax
import numpy as np

D_MODEL = 2048
BATCH = 8
SEQ = 4096
DEPTH = 2
DEC_BATCH = 8
DEC_SEQ = 2048
PAST_LEN = 128

F32 = jnp.float32
EPS = 1e-6
GRID_W = 64
HEAD_DIM = 128
ATT_HEADS = 4
ATT_KV_HEADS = 2
ATT_GROUP = ATT_HEADS // ATT_KV_HEADS
Q_BLOCK = 128
ROPE_THETA = 10000.0
NA_HEADS = 4
NA_WIN_ROWS = 8
NA_WIN_COLS = 16
NA_QCOLS = 16
NA_KCOLS = 32
NA_NCB = GRID_W // NA_QCOLS
DN_HEADS = 8
DN_KDIM = 128
DN_VDIM = 128
CONV_K = 5
CHUNK = 64
A_Q = ATT_HEADS * HEAD_DIM
A_KV = ATT_KV_HEADS * HEAD_DIM
B_W = NA_HEADS * HEAD_DIM
C_K = DN_HEADS * DN_KDIM
C_V = DN_HEADS * DN_VDIM
CONV_CH = 2 * C_K + C_V
SPLITS = (A_Q, A_KV, A_KV, B_W, B_W, B_W, C_K, C_K, C_V, C_V, 4 * DN_HEADS)
PROJ_WIDTH = 6688
MIX_WIDTH = A_Q + B_W + C_V
D_FF = 5632
N_EXPERTS = 8
TOP_K = 2
D_FF_EXPERT = 7168
N_DENSE = (DEPTH + 1) // 2
N_MOE = DEPTH // 2

kernel_name = "hybrid_bidir_gqa_natten_gdn_encoder"


def rms_norm(x, w):
    xf = x.astype(F32)
    y = xf * lax.rsqrt(jnp.mean(xf * xf, axis=-1, keepdims=True) + EPS)
    return (y * w.astype(F32)).astype(x.dtype)


def l2_norm(x):
    return x * lax.rsqrt(jnp.sum(x * x, axis=-1, keepdims=True) + EPS)


def axial_rope(x):
    n_tok = x.shape[1]
    half = HEAD_DIM // 2
    quarter = half // 2
    t = jnp.arange(n_tok)
    inv = ROPE_THETA ** (-jnp.arange(quarter, dtype=F32) / quarter)
    ang_r = (t // GRID_W).astype(F32)[:, None] * inv
    ang_c = (t % GRID_W).astype(F32)[:, None] * inv

    def rot(seg, ang):
        cos = jnp.cos(ang)[None, :, None, :]
        sin = jnp.sin(ang)[None, :, None, :]
        s1, s2 = seg[..., :quarter], seg[..., quarter:]
        return jnp.concatenate([s1 * cos - s2 * sin, s2 * cos + s1 * sin], axis=-1)

    xf = x.astype(F32)
    return jnp.concatenate([rot(xf[..., :half], ang_r), rot(xf[..., half:], ang_c)], axis=-1).astype(x.dtype)


def global_gqa(q, k, v, q_norm, k_norm):
    B, S, _ = q.shape
    q = axial_rope(rms_norm(q.reshape(B, S, ATT_HEADS, HEAD_DIM), q_norm))
    k = axial_rope(rms_norm(k.reshape(B, S, ATT_KV_HEADS, HEAD_DIM), k_norm))
    v = v.reshape(B, S, ATT_KV_HEADS, HEAD_DIM)
    nb = S // Q_BLOCK
    qb = q.reshape(B, nb, Q_BLOCK, ATT_KV_HEADS, ATT_GROUP, HEAD_DIM).transpose(1, 0, 2, 3, 4, 5)
    scale = HEAD_DIM ** -0.5

    def block(qi):
        s = jnp.einsum('bqkgd,bskd->bkgqs', qi, k).astype(F32) * scale
        p = jax.nn.softmax(s, axis=-1).astype(v.dtype)
        return jnp.einsum('bkgqs,bskd->bqkgd', p, v)

    o = lax.map(block, qb)
    return o.transpose(1, 0, 2, 3, 4, 5).reshape(B, S, A_Q)


def neighbourhood_attn(q, k, v, rpb):
    B, S, _ = q.shape
    rows = S // GRID_W
    wr = min(NA_WIN_ROWS, rows)
    qg = q.reshape(B, rows, GRID_W, NA_HEADS, HEAD_DIM)
    kg = k.reshape(B, rows, GRID_W, NA_HEADS, HEAD_DIM)
    vg = v.reshape(B, rows, GRID_W, NA_HEADS, HEAD_DIM)
    qc = np.arange(GRID_W).reshape(NA_NCB, NA_QCOLS)
    kstart = np.clip(np.arange(NA_NCB) * NA_QCOLS - NA_WIN_COLS // 2, 0, GRID_W - NA_KCOLS)
    kc = kstart[:, None] + np.arange(NA_KCOLS)
    cs = np.clip(qc - NA_WIN_COLS // 2, 0, GRID_W - NA_WIN_COLS)
    col_ok = (kc[:, None, :] >= cs[..., None]) & (kc[:, None, :] < cs[..., None] + NA_WIN_COLS)
    dc_idx = np.clip(kc[:, None, :] - qc[..., None], -(NA_WIN_COLS - 1), NA_WIN_COLS - 1) + NA_WIN_COLS - 1
    col_bias = rpb.astype(F32)[:, :, dc_idx]
    neg = jnp.where(jnp.asarray(col_ok), 0.0, -jnp.inf).astype(F32)
    scale = HEAD_DIM ** -0.5

    def row_block(args):
        r, q_row = args
        rs = jnp.clip(r - wr // 2, 0, rows - wr)
        k_blk = lax.dynamic_slice_in_dim(kg, rs, wr, axis=1)[:, :, kc]
        v_blk = lax.dynamic_slice_in_dim(vg, rs, wr, axis=1)[:, :, kc]
        dr_idx = rs + jnp.arange(wr) - r + NA_WIN_ROWS - 1
        bias = jnp.take(col_bias, dr_idx, axis=1).transpose(0, 2, 3, 1, 4)
        qr = q_row.reshape(B, NA_NCB, NA_QCOLS, NA_HEADS, HEAD_DIM)
        s = jnp.einsum('bjqhd,bijmhd->bhjqim', qr, k_blk).astype(F32) * scale
        s = s + bias[None] + neg[None, None, :, :, None, :]
        p = jax.nn.softmax(s.reshape(B, NA_HEADS, NA_NCB, NA_QCOLS, wr * NA_KCOLS), axis=-1)
        p = p.reshape(s.shape).astype(v.dtype)
        o = jnp.einsum('bhjqim,bijmhd->bjqhd', p, v_blk)
        return o.reshape(B, GRID_W, B_W)

    o = lax.map(row_block, (jnp.arange(rows), qg.transpose(1, 0, 2, 3, 4)))
    return o.transpose(1, 0, 2, 3).reshape(B, S, B_W)


def chunked_gated_delta(q, k, v, g, beta):
    B, S, H, dk = q.shape
    dv = v.shape[-1]
    n = S // CHUNK

    def to_chunks(t):
        return t.astype(F32).reshape(B, n, CHUNK, H, -1).transpose(0, 3, 1, 2, 4)

    q = to_chunks(q) * (dk ** -0.5)
    k = to_chunks(k)
    v = to_chunks(v)
    g = g.astype(F32).reshape(B, n, CHUNK, H).transpose(0, 3, 1, 2)
    beta = beta.astype(F32).reshape(B, n, CHUNK, H).transpose(0, 3, 1, 2)
    gc = jnp.cumsum(g, axis=-1)
    tril = jnp.tril(jnp.ones((CHUNK, CHUNK), dtype=bool))
    strict = jnp.tril(jnp.ones((CHUNK, CHUNK), dtype=bool), -1)
    decay = jnp.exp(jnp.where(tril, gc[..., :, None] - gc[..., None, :], -jnp.inf))
    kb = k * beta[..., None]
    m = jnp.einsum('bhncd,bhnsd->bhncs', kb, k) * decay
    lmat = jnp.where(strict, m, 0.0) + jnp.eye(CHUNK, dtype=F32)
    rhs = jnp.concatenate([v * beta[..., None], kb * jnp.exp(gc)[..., None]], axis=-1)
    sol = lax.linalg.triangular_solve(lmat, rhs, left_side=True, lower=True, unit_diagonal=True)
    u, w = sol[..., :dv], sol[..., dv:]
    qk = jnp.einsum('bhncd,bhnsd->bhncs', q, k) * decay
    q_dec = q * jnp.exp(gc)[..., None]
    k_dec = k * jnp.exp(gc[..., -1:] - gc)[..., None]
    g_last = jnp.exp(gc[..., -1])

    def step(state, xs):
        qk_c, qd_c, u_c, w_c, kd_c, gl_c = xs
        v_new = u_c - jnp.einsum('bhcd,bhde->bhce', w_c, state)
        o = jnp.einsum('bhcd,bhde->bhce', qd_c, state) + jnp.einsum('bhcs,bhse->bhce', qk_c, v_new)
        state = state * gl_c[..., None, None] + jnp.einsum('bhcd,bhce->bhde', kd_c, v_new)
        return state, o

    xs = tuple(jnp.moveaxis(t, 2, 0) for t in (qk, q_dec, u, w, k_dec, g_last))
    s0 = jnp.zeros((B, H, dk, dv), F32)
    _, o = lax.scan(step, s0, xs)
    return o.transpose(1, 0, 3, 2, 4).reshape(B, S, H, dv)


def gated_deltanet(q, k, v, z, ab, conv_w, a_log, dt_bias, out_norm):
    B, S, _ = q.shape
    qkv = jnp.concatenate([q, k, v], axis=-1)
    qkv = lax.conv_general_dilated(
        qkv, conv_w[:, None, :].astype(qkv.dtype), window_strides=(1,),
        padding=[(CONV_K // 2, CONV_K // 2)], dimension_numbers=('NWC', 'WIO', 'NWC'),
        feature_group_count=CONV_CH)
    qkv = jax.nn.silu(qkv)
    q, k, v = jnp.split(qkv, [C_K, 2 * C_K], axis=-1)
    q = l2_norm(q.astype(F32).reshape(B, S, DN_HEADS, DN_KDIM))
    k = l2_norm(k.astype(F32).reshape(B, S, DN_HEADS, DN_KDIM))
    v = v.astype(F32).reshape(B, S, DN_HEADS, DN_VDIM)
    ab = ab.astype(F32).reshape(B, S, 4, DN_HEADS)
    g = -jnp.exp(a_log.astype(F32)) * jax.nn.softplus(ab[:, :, 0:2] + dt_bias.astype(F32))
    beta = jax.nn.sigmoid(ab[:, :, 2:4])
    o_f = chunked_gated_delta(q, k, v, g[:, :, 0], beta[:, :, 0])
    fl = lambda t: jnp.flip(t, axis=1)
    o_b = fl(chunked_gated_delta(fl(q), fl(k), fl(v), fl(g[:, :, 1]), fl(beta[:, :, 1])))
    o = rms_norm(o_f + o_b, out_norm) * jax.nn.silu(z.astype(F32).reshape(B, S, DN_HEADS, DN_VDIM))
    return o.reshape(B, S, C_V).astype(z.dtype)


def swiglu(h, w_gate, w_up, w_down):
    return (jax.nn.silu(h @ w_gate) * (h @ w_up)) @ w_down


def moe_ffn(h, w_router, w_gate, w_up, w_down):
    B, S, D = h.shape
    t = h.reshape(B * S, D)
    logits = (t @ w_router).astype(F32)
    top_v, top_i = lax.top_k(logits, TOP_K)
    gates = jax.nn.softmax(top_v, axis=-1)
    comb = jnp.sum(jax.nn.one_hot(top_i, N_EXPERTS, dtype=F32) * gates[..., None], axis=1)
    out = jnp.zeros_like(t)
    for e in range(N_EXPERTS):
        out = out + comb[:, e:e + 1].astype(t.dtype) * swiglu(t, w_gate[e], w_up[e], w_down[e])
    return out.reshape(B, S, D)


def trunk(x, mix_norm, w_in, q_norm, k_norm, rpb, conv_w, a_log, dt_bias, out_norm, w_out,
          ffn_norm, dense_gate, dense_up, dense_down, router, moe_gate, moe_up, moe_down, final_norm):
    split_idx = np.cumsum(SPLITS)[:-1].tolist()
    for l in range(DEPTH):
        h = rms_norm(x, mix_norm[l])
        aq, ak, av, bq, bk, bv, cq, ck, cv, cz, cab = jnp.split(h @ w_in[l], split_idx, axis=-1)
        ya = global_gqa(aq, ak, av, q_norm[l], k_norm[l])
        yb = neighbourhood_attn(bq, bk, bv, rpb[l])
        yc = gated_deltanet(cq, ck, cv, cz, cab, conv_w[l], a_log[l], dt_bias[l], out_norm[l])
        x = x + jnp.concatenate([ya, yb, yc], axis=-1) @ w_out[l]
        h = rms_norm(x, ffn_norm[l])
        if l % 2 == 0:
            x = x + swiglu(h, dense_gate[l // 2], dense_up[l // 2], dense_down[l // 2])
        else:
            x = x + moe_ffn(h, router[l // 2], moe_gate[l // 2], moe_up[l // 2], moe_down[l // 2])
    return rms_norm(x, final_norm)


def setup_inputs(seed: int = 0) -> dict:
    key = jax.random.key(seed)
    ks = jax.random.split(key, 24)

    def nrm(k, shape, fan_in):
        return jax.random.normal(k, shape, F32) * (fan_in ** -0.5)

    def gain(k, shape):
        return 1.0 + 0.05 * jax.random.normal(k, shape, F32)

    dt = jnp.exp(jax.random.uniform(ks[9], (DEPTH, 2, DN_HEADS), F32, math.log(1e-3), math.log(1e-1)))
    return {
        'x_prompt': jax.random.normal(ks[0], (BATCH, SEQ, D_MODEL), F32),
        'x_sample': jax.random.normal(ks[1], (DEC_BATCH, DEC_SEQ, D_MODEL), F32),
        'mix_norm': gain(ks[2], (DEPTH, D_MODEL)),
        'w_in': nrm(ks[3], (DEPTH, D_MODEL, PROJ_WIDTH), D_MODEL),
        'q_norm': gain(ks[4], (DEPTH, HEAD_DIM)),
        'k_norm': gain(ks[5], (DEPTH, HEAD_DIM)),
        'rpb': 0.1 * jax.random.normal(ks[6], (DEPTH, NA_HEADS, 2 * NA_WIN_ROWS - 1, 2 * NA_WIN_COLS - 1), F32),
        'conv_w': nrm(ks[7], (DEPTH, CONV_K, CONV_CH), CONV_K),
        'a_log': jnp.log(jax.random.uniform(ks[8], (DEPTH, 2, DN_HEADS), F32, 1.0, 16.0)),
        'dt_bias': dt + jnp.log(-jnp.expm1(-dt)),
        'out_norm': gain(ks[10], (DEPTH, DN_VDIM)),
        'w_out': nrm(ks[11], (DEPTH, MIX_WIDTH, D_MODEL), MIX_WIDTH),
        'ffn_norm': gain(ks[12], (DEPTH, D_MODEL)),
        'dense_gate': nrm(ks[13], (N_DENSE, D_MODEL, D_FF), D_MODEL),
        'dense_up': nrm(ks[14], (N_DENSE, D_MODEL, D_FF), D_MODEL),
        'dense_down': nrm(ks[15], (N_DENSE, D_FF, D_MODEL), D_FF),
        'router': nrm(ks[16], (N_MOE, D_MODEL, N_EXPERTS), D_MODEL),
        'moe_gate': nrm(ks[17], (N_MOE, N_EXPERTS, D_MODEL, D_FF_EXPERT), D_MODEL),
        'moe_up': nrm(ks[18], (N_MOE, N_EXPERTS, D_MODEL, D_FF_EXPERT), D_MODEL),
        'moe_down': nrm(ks[19], (N_MOE, N_EXPERTS, D_FF_EXPERT, D_MODEL), D_FF_EXPERT),
        'final_norm': gain(ks[20], (D_MODEL,)),
    }


def reference(x_prompt, x_sample, mix_norm, w_in, q_norm, k_norm, rpb, conv_w, a_log, dt_bias, out_norm,
              w_out, ffn_norm, dense_gate, dense_up, dense_down, router, moe_gate, moe_up, moe_down, final_norm):
    y_prompt = trunk(x_prompt, mix_norm, w_in, q_norm, k_norm, rpb, conv_w, a_log, dt_bias, out_norm, w_out,
                     ffn_norm, dense_gate, dense_up, dense_down, router, moe_gate, moe_up, moe_down, final_norm)
    y_sample = trunk(x_sample, mix_norm, w_in, q_norm, k_norm, rpb, conv_w, a_log, dt_bias, out_norm, w_out,
                     ffn_norm, dense_gate, dense_up, dense_down, router, moe_gate, moe_up, moe_down, final_norm)
    return (y_prompt, y_sample)
```

```python
import functools

import numpy as np
import jax
import jax.numpy as jnp
from jax import lax
from jax.experimental import pallas as pl
from jax.experimental.pallas import tpu as pltpu

F32 = jnp.float32
BF16 = jnp.bfloat16
EPS = 1e-6

D_MODEL = 2048
GRID_W = 64
HEAD_DIM = 128
ATT_HEADS = 4
ATT_KV_HEADS = 2
ROPE_THETA = 10000.0
NA_HEADS = 4
NA_WIN_ROWS = 8
NA_WIN_COLS = 16
DN_HEADS = 8
CONV_K = 5
CHUNK = 64
N_EXPERTS = 8
LANES = 128

PROJ_MAIN = 6656
REF_SPLITS = (512, 256, 256, 512, 512, 512, 1024, 1024, 1024, 1024, 32)
REF_NAMES = ("aq", "ak", "av", "bq", "bk", "bv", "cq", "ck", "cv", "cz", "cab")
MY_ORDER = ("cq", "ck", "cv", "cz", "aq", "bq", "bk", "bv", "ak", "av")
COL = {}
_off = 0
for _n in MY_ORDER:
    COL[_n] = _off
    _off += REF_SPLITS[REF_NAMES.index(_n)]
assert _off == PROJ_MAIN

MIB = 1024 * 1024


def _cp(sem, vmem_mib):
    return pltpu.CompilerParams(dimension_semantics=sem, vmem_limit_bytes=vmem_mib * MIB)


def _inproj_kernel(x_ref, nw_ref, w_ref, wc_ref, o_ref, oc_ref, h_ref):
    @pl.when(pl.program_id(1) == 0)
    def _():
        x = x_ref[...]
        ms = jnp.mean(x * x, axis=-1, keepdims=True)
        hb = (x * lax.rsqrt(ms + EPS) * nw_ref[...]).astype(BF16)
        h_ref[...] = hb
        oc_ref[...] = jnp.dot(hb, wc_ref[...], preferred_element_type=F32)

    o_ref[...] = jnp.dot(h_ref[...], w_ref[...], preferred_element_type=F32).astype(o_ref.dtype)


def _inproj(x, nw, w, wc, tm=512, tn=512):
    T = x.shape[0]
    return pl.pallas_call(
        _inproj_kernel,
        grid=(T // tm, PROJ_MAIN // tn),
        in_specs=[
            pl.BlockSpec((tm, D_MODEL), lambda i, j: (i, 0)),
            pl.BlockSpec((1, D_MODEL), lambda i, j: (0, 0)),
            pl.BlockSpec((D_MODEL, tn), lambda i, j: (0, j)),
            pl.BlockSpec((D_MODEL, LANES), lambda i, j: (0, 0)),
        ],
        out_specs=[
            pl.BlockSpec((tm, tn), lambda i, j: (i, j)),
            pl.BlockSpec((tm, LANES), lambda i, j: (i, 0)),
        ],
        out_shape=[jax.ShapeDtypeStruct((T, PROJ_MAIN), BF16), jax.ShapeDtypeStruct((T, LANES), F32)],
        scratch_shapes=[pltpu.VMEM((tm, D_MODEL), BF16)],
        compiler_params=_cp(("parallel", "arbitrary"), 40),
        name="inproj",
    )(x, nw, w, wc)


def _rope_tables(S):
    half = HEAD_DIM // 2
    quarter = half // 2
    t = jnp.arange(S)
    inv = ROPE_THETA ** (-jnp.arange(quarter, dtype=F32) / quarter)
    ang_r = (t // GRID_W).astype(F32)[:, None] * inv
    ang_c = (t % GRID_W).astype(F32)[:, None] * inv
    cos = jnp.concatenate([jnp.cos(ang_r), jnp.cos(ang_r), jnp.cos(ang_c), jnp.cos(ang_c)], axis=-1)
    sin = jnp.concatenate([-jnp.sin(ang_r), jnp.sin(ang_r), -jnp.sin(ang_c), jnp.sin(ang_c)], axis=-1)
    return cos, sin


def _norm_rope(xh, nw, cos, sin, first_quarter):
    ms = jnp.mean(xh * xh, axis=-1, keepdims=True)
    xn = xh * lax.rsqrt(ms + EPS) * nw
    partner = jnp.where(first_quarter, pltpu.roll(xn, 96, 1), pltpu.roll(xn, 32, 1))
    return xn * cos + partner * sin


def _attn_prep_kernel(q_ref, k_ref, cos_ref, sin_ref, qn_ref, kn_ref, qo_ref, ko_ref):
    cos = cos_ref[...]
    sin = sin_ref[...]
    lane = lax.broadcasted_iota(jnp.int32, cos.shape, 1)
    first_quarter = (lane % 64) < 32
    scale = HEAD_DIM ** -0.5
    for h in range(ATT_HEADS):
        sl = slice(h * HEAD_DIM, (h + 1) * HEAD_DIM)
        y = _norm_rope(q_ref[:, sl].astype(F32), qn_ref[...], cos, sin, first_quarter)
        qo_ref[:, sl] = (y * scale).astype(BF16)
    for h in range(ATT_KV_HEADS):
        sl = slice(h * HEAD_DIM, (h + 1) * HEAD_DIM)
        y = _norm_rope(k_ref[:, sl].astype(F32), kn_ref[...], cos, sin, first_quarter)
        ko_ref[:, sl] = y.astype(BF16)


def _attn_prep(proj, cos, sin, qn, kn, base, B, S, tm=256):
    nb = S // tm
    rb = base // tm
    return pl.pallas_call(
        _attn_prep_kernel,
        grid=(B * nb,),
        in_specs=[
            pl.BlockSpec((tm, 512), lambda i: (rb + i, COL["aq"] // 512)),
            pl.BlockSpec((tm, 256), lambda i: (rb + i, COL["ak"] // 256)),
            pl.BlockSpec((tm, LANES), lambda i: (i % nb, 0)),
            pl.BlockSpec((tm, LANES), lambda i: (i % nb, 0)),
            pl.BlockSpec((1, LANES), lambda i: (0, 0)),
            pl.BlockSpec((1, LANES), lambda i: (0, 0)),
        ],
        out_specs=[
            pl.BlockSpec((tm, 512), lambda i: (i, 0)),
            pl.BlockSpec((tm, 256), lambda i: (i, 0)),
        ],
        out_shape=[jax.ShapeDtypeStruct((B * S, 512), BF16), jax.ShapeDtypeStruct((B * S, 256), BF16)],
        compiler_params=_cp(("parallel",), 32),
        name="attn_prep",
    )(proj, proj, cos, sin, qn, kn)


def _flash_kernel(q_ref, k_ref, v_ref, o_ref, m_ref, l_ref, acc_ref, *, tq, tk, S):
    q = q_ref[...]
    q2 = jnp.concatenate([q[:, :HEAD_DIM], q[:, HEAD_DIM:]], axis=0)
    m_ref[...] = jnp.full(m_ref.shape, -jnp.inf, F32)
    l_ref[...] = jnp.zeros(l_ref.shape, F32)
    acc_ref[...] = jnp.zeros(acc_ref.shape, F32)

    def body(j, carry):
        start = pl.multiple_of(j * tk, tk)
        k = k_ref[pl.ds(start, tk), :]
        v = v_ref[pl.ds(start, tk), :]
        s = lax.dot_general(q2, k, (((1,), (1,)), ((), ())), preferred_element_type=F32)
        m_prev = m_ref[...]
        m_new = jnp.maximum(m_prev, jnp.max(s, axis=-1, keepdims=True))
        alpha = jnp.exp(m_prev - m_new)
        p = jnp.exp(s - m_new[:, :1])
        l_ref[...] = alpha * l_ref[...] + jnp.sum(p, axis=-1, keepdims=True)
        acc_ref[...] = alpha * acc_ref[...] + jnp.dot(p.astype(BF16), v, preferred_element_type=F32)
        m_ref[...] = m_new
        return carry

    lax.fori_loop(0, S // tk, body, 0)
    o = acc_ref[...] / l_ref[...]
    o_ref[...] = jnp.concatenate([o[:tq], o[tq:]], axis=1).astype(o_ref.dtype)


def _flash(qa, ka, proj, base, B, S, tq=256, tk=512):
    nq = S // tq
    vb = COL["av"] // HEAD_DIM
    return pl.pallas_call(
        functools.partial(_flash_kernel, tq=tq, tk=tk, S=S),
        grid=(B, ATT_KV_HEADS, nq),
        in_specs=[
            pl.BlockSpec((tq, 256), lambda b, h, i: (b * nq + i, h)),
            pl.BlockSpec((S, HEAD_DIM), lambda b, h, i: (b, h)),
            pl.BlockSpec((S, HEAD_DIM), lambda b, h, i: (base // S + b, vb + h)),
        ],
        out_specs=pl.BlockSpec((tq, 256), lambda b, h, i: (b * nq + i, h)),
        out_shape=jax.ShapeDtypeStruct((B * S, 512), BF16),
        scratch_shapes=[
            pltpu.VMEM((2 * tq, HEAD_DIM), F32),
            pltpu.VMEM((2 * tq, HEAD_DIM), F32),
            pltpu.VMEM((2 * tq, HEAD_DIM), F32),
        ],
        compiler_params=_cp(("parallel", "parallel", "arbitrary"), 32),
        name="flash_gqa",
    )(qa, ka, proj)


NA_QROWS = 8
NA_KROWS = 16
NA_TQ = NA_QROWS * GRID_W
NA_TK = NA_KROWS * GRID_W
NA_NEG = -1e30


def _na_bias_tables(rpb):
    qr = np.arange(NA_QROWS)[:, None, None, None]
    c = np.arange(GRID_W)[None, :, None, None]
    kr = np.arange(NA_KROWS)[None, None, :, None]
    kc = np.arange(GRID_W)[None, None, None, :]
    cs = np.clip(c - NA_WIN_COLS // 2, 0, GRID_W - NA_WIN_COLS)
    col_ok = (kc >= cs) & (kc < cs + NA_WIN_COLS)
    dc = np.clip(kc - c, -(NA_WIN_COLS - 1), NA_WIN_COLS - 1) + NA_WIN_COLS - 1
    tables = []
    for off, lo in ((0, np.maximum(qr - 4, 0)), (-4, qr + 0), (-8, 8 + np.minimum(qr - 4, 0))):
        row_ok = (kr >= lo) & (kr < lo + NA_WIN_ROWS)
        dr = np.clip(off + kr - qr + NA_WIN_ROWS - 1, 0, 2 * NA_WIN_ROWS - 2)
        shape = (NA_QROWS, GRID_W, NA_KROWS, GRID_W)
        dr_b = np.broadcast_to(dr, shape)
        dc_b = np.broadcast_to(dc, shape)
        ok = np.broadcast_to(row_ok & col_ok, shape)
        bias = rpb.astype(F32)[:, dr_b, dc_b]
        tables.append(jnp.where(jnp.asarray(ok)[None], bias, NA_NEG).reshape(NA_HEADS, NA_TQ, NA_TK))
    return jnp.stack(tables)


def _na_kernel(q_ref, k_ref, v_ref, b_ref, o_ref, *, S):
    j = pl.program_id(2)
    start = pl.multiple_of(jnp.clip(j * NA_TQ - 4 * GRID_W, 0, S - NA_TK), 4 * GRID_W)
    k = k_ref[pl.ds(start, NA_TK), :]
    v = v_ref[pl.ds(start, NA_TK), :]
    s = lax.dot_general(q_ref[...], k, (((1,), (1,)), ((), ())), preferred_element_type=F32)
    s = s * (HEAD_DIM ** -0.5) + b_ref[0, 0]
    m = jnp.max(s, axis=-1, keepdims=True)
    p = jnp.exp(s - m)
    l = jnp.sum(p, axis=-1, keepdims=True)
    o = jnp.dot(p.astype(BF16), v, preferred_element_type=F32)
    o_ref[...] = (o / l).astype(o_ref.dtype)


def _na(proj, bias, base, B, S):
    assert S % NA_TQ == 0 and S >= NA_TK
    nj = S // NA_TQ
    qb, kb, vb = COL["bq"] // HEAD_DIM, COL["bk"] // HEAD_DIM, COL["bv"] // HEAD_DIM

    def case(j):
        return jnp.where(j == 0, 0, jnp.where(j == nj - 1, 2, 1))

    return pl.pallas_call(
        functools.partial(_na_kernel, S=S),
        grid=(B, NA_HEADS, nj),
        in_specs=[
            pl.BlockSpec((NA_TQ, HEAD_DIM), lambda b, h, j: (base // NA_TQ + b * nj + j, qb + h)),
            pl.BlockSpec((S, HEAD_DIM), lambda b, h, j: (base // S + b, kb + h)),
            pl.BlockSpec((S, HEAD_DIM), lambda b, h, j: (base // S + b, vb + h)),
            pl.BlockSpec((1, 1, NA_TQ, NA_TK), lambda b, h, j: (case(j), h, 0, 0)),
        ],
        out_specs=pl.BlockSpec((NA_TQ, HEAD_DIM), lambda b, h, j: (b * nj + j, h)),
        out_shape=jax.ShapeDtypeStruct((B * S, NA_HEADS * HEAD_DIM), BF16),
        compiler_params=_cp(("parallel", "parallel", "arbitrary"), 40),
        name="nbr_attn",
    )(proj, proj, proj, bias)


G_GC, G_BETA, G_EGC, G_EKD, G_ELAST, G_BEXP = 0, 16, 32, 48, 64, 80
GDN_TB = 256
HALO = 16


def _split3(x):
    a = x.astype(BF16)
    r = x - a.astype(F32)
    b = r.astype(BF16)
    c = (r - b.astype(F32)).astype(BF16)
    return a, b, c


def _dot_exact_lhs(m_bf16, x):
    a, b, c = _split3(x)
    d = functools.partial(jnp.dot, preferred_element_type=F32)
    return d(m_bf16, a) + d(m_bf16, b) + d(m_bf16, c)


def _gdn_prep_kernel(x_ref, xp_ref, xn_ref, cw_ref, cab_ref, nega_ref, dtb_ref,
                     q_ref, k_ref, v_ref, g_ref, ext_ref, *, nb):
    i = pl.program_id(0)
    tb = GDN_TB
    first = (i % nb) == 0
    last = (i % nb) == nb - 1
    ext_ref[0:HALO, :] = jnp.where(first, 0.0, xp_ref[...].astype(F32))
    ext_ref[HALO:HALO + tb, :] = x_ref[...].astype(F32)
    ext_ref[HALO + tb:, :] = jnp.where(last, 0.0, xn_ref[...].astype(F32))
    outs = (q_ref, k_ref, v_ref)
    for c in range(3 * DN_HEADS):
        sl = slice(c * LANES, (c + 1) * LANES)
        acc = jnp.zeros((tb, LANES), F32)
        for t in range(CONV_K):
            acc = acc + cw_ref[t:t + 1, sl] * ext_ref[HALO - CONV_K // 2 + t:HALO - CONV_K // 2 + t + tb, sl]
        y = acc * jax.nn.sigmoid(acc)
        which, h = divmod(c, DN_HEADS)
        if which < 2:
            y = y * lax.rsqrt(jnp.sum(y * y, axis=-1, keepdims=True) + EPS)
        if which == 0:
            y = y * (HEAD_DIM ** -0.5)
        outs[which][:, h * LANES:(h + 1) * LANES] = y.astype(BF16)

    cab = cab_ref[...]
    lane = lax.broadcasted_iota(jnp.int32, cab.shape, 1)
    z = cab + dtb_ref[...]
    softplus = jnp.maximum(z, 0.0) + jnp.log1p(jnp.exp(-jnp.abs(z)))
    g = jnp.where(lane < 16, nega_ref[...] * softplus, 0.0)
    beta = jnp.where((lane >= 16) & (lane < 32), jax.nn.sigmoid(cab), 0.0)
    r = lax.broadcasted_iota(jnp.int32, (tb, tb), 0)
    cidx = lax.broadcasted_iota(jnp.int32, (tb, tb), 1)
    same = (r // CHUNK) == (cidx // CHUNK)
    allm32 = jnp.where(same, 1.0, 0.0).astype(F32)
    low = jnp.where(cidx <= r, allm32, 0.0).astype(BF16)
    upp = jnp.where(cidx >= r, allm32, 0.0).astype(BF16)
    allm = allm32.astype(BF16)
    gc = jnp.where(lane < 8, _dot_exact_lhs(low, g), _dot_exact_lhs(upp, g))
    tot = _dot_exact_lhs(allm, g)
    egc = jnp.exp(gc)
    out = gc
    out = out + beta
    out = out + pltpu.roll(jnp.where(lane < 16, egc, 0.0), G_EGC, 1)
    out = out + pltpu.roll(jnp.where(lane < 16, jnp.exp(tot - gc), 0.0), G_EKD, 1)
    out = out + pltpu.roll(jnp.where(lane < 16, jnp.exp(tot), 0.0), G_ELAST, 1)
    bexp = beta * pltpu.roll(jnp.where(lane < 16, egc, 0.0), 16, 1)
    out = out + pltpu.roll(bexp, G_BEXP - 16, 1)
    g_ref[...] = out


def _gdn_prep(proj, cab, conv_w, nega, dtb, base, B, S):
    tb = GDN_TB
    nb = S // tb
    rb = base // tb
    hb = tb // HALO
    nrows = proj.shape[0] // HALO
    width = 3 * DN_HEADS * LANES
    return pl.pallas_call(
        functools.partial(_gdn_prep_kernel, nb=nb),
        grid=(B * nb,),
        in_specs=[
            pl.BlockSpec((tb, width), lambda i: (rb + i, 0)),
            pl.BlockSpec((HALO, width), lambda i: (jnp.maximum((rb + i) * hb - 1, 0), 0)),
            pl.BlockSpec((HALO, width), lambda i: (jnp.minimum((rb + i + 1) * hb, nrows - 1), 0)),
            pl.BlockSpec((CONV_K, width), lambda i: (0, 0)),
            pl.BlockSpec((tb, LANES), lambda i: (rb + i, 0)),
            pl.BlockSpec((1, LANES), lambda i: (0, 0)),
            pl.BlockSpec((1, LANES), lambda i: (0, 0)),
        ],
        out_specs=[
            pl.BlockSpec((tb, 1024), lambda i: (i, 0)),
            pl.BlockSpec((tb, 1024), lambda i: (i, 0)),
            pl.BlockSpec((tb, 1024), lambda i: (i, 0)),
            pl.BlockSpec((tb, LANES), lambda i: (i, 0)),
        ],
        out_shape=[jax.ShapeDtypeStruct((B * S, 1024), BF16)] * 3 + [jax.ShapeDtypeStruct((B * S, LANES), F32)],
        scratch_shapes=[pltpu.VMEM((tb + 2 * HALO, width), F32)],
        compiler_params=_cp(("parallel",), 40),
        name="gdn_prep",
    )(proj, proj, proj, conv_w, cab, nega, dtb)


def _gdn_scan_kernel(q_ref, k_ref, v_ref, g_ref, gt_ref, o_ref, s_ref, *, reverse):
    @pl.when(pl.program_id(1) == 0)
    def _():
        s_ref[...] = jnp.zeros(s_ref.shape, F32)

    C = CHUNK
    d = 1 if reverse else 0
    gcol = g_ref[...]
    grow = gt_ref[0]
    ri = lax.broadcasted_iota(jnp.int32, (C, C), 0)
    ci = lax.broadcasted_iota(jnp.int32, (C, C), 1)
    incl = (ri <= ci) if reverse else (ri >= ci)
    strict = (ri < ci) if reverse else (ri > ci)
    eye = jnp.where(ri == ci, 1.0, 0.0).astype(F32)
    dot = functools.partial(jnp.dot, preferred_element_type=F32)

    def colb(field, h, width):
        idx = field + d * DN_HEADS + h
        return jnp.broadcast_to(gcol[:, idx:idx + 1], (C, width))

    def rowb(field, h):
        idx = field + d * DN_HEADS + h
        return jnp.broadcast_to(grow[idx:idx + 1, :], (C, C))

    for h in range(DN_HEADS):
        sl = slice(h * LANES, (h + 1) * LANES)
        q = q_ref[:, sl]
        k = k_ref[:, sl]
        v = v_ref[:, sl]
        decay = jnp.where(incl, jnp.exp(colb(G_GC, h, C) - rowb(G_GC, h)), 0.0)
        kq = lax.dot_general(jnp.concatenate([k, q], axis=0), k, (((1,), (1,)), ((), ())),
                             preferred_element_type=F32)
        n = jnp.where(strict, -(kq[:C] * colb(G_BETA, h, C) * decay), 0.0)
        qk = kq[C:] * decay
        x = eye + n
        nb16 = n.astype(BF16)
        p = dot(nb16, nb16)
        steps = int(np.log2(C)) - 1
        for it in range(steps):
            pb = p.astype(BF16)
            if it < steps - 1:
                xp = dot(pb, jnp.concatenate([x.astype(BF16), pb], axis=1))
                x = x + xp[:, :C]
                p = xp[:, C:]
            else:
                x = x + dot(pb, x.astype(BF16))
        u = dot((x * rowb(G_BETA, h)).astype(BF16), v)
        w = dot((x * rowb(G_BEXP, h)).astype(BF16), k)
        sb = s_ref[h].astype(BF16)
        v_new = u - dot(w.astype(BF16), sb)
        vb = v_new.astype(BF16)
        o = colb(G_EGC, h, LANES) * dot(q, sb) + dot(qk.astype(BF16), vb)
        kd = (k.astype(F32) * colb(G_EKD, h, LANES)).T.astype(BF16)
        idx = G_ELAST + d * DN_HEADS + h
        glast = jnp.broadcast_to(gcol[0:1, idx:idx + 1], (LANES, LANES))
        s_ref[h] = s_ref[h] * glast + dot(kd, vb)
        o_ref[:, sl] = o


def _gdn_scan(qn, kn, vc, gates, gates_t, B, S, reverse):
    n = S // CHUNK

    def idx(b, c):
        return b * n + ((n - 1 - c) if reverse else c)

    return pl.pallas_call(
        functools.partial(_gdn_scan_kernel, reverse=reverse),
        grid=(B, n),
        in_specs=[
            pl.BlockSpec((CHUNK, 1024), lambda b, c: (idx(b, c), 0)),
            pl.BlockSpec((CHUNK, 1024), lambda b, c: (idx(b, c), 0)),
            pl.BlockSpec((CHUNK, 1024), lambda b, c: (idx(b, c), 0)),
            pl.BlockSpec((CHUNK, LANES), lambda b, c: (idx(b, c), 0)),
            pl.BlockSpec((1, LANES, CHUNK), lambda b, c: (idx(b, c), 0, 0)),
        ],
        out_specs=pl.BlockSpec((CHUNK, 1024), lambda b, c: (idx(b, c), 0)),
        out_shape=jax.ShapeDtypeStruct((B * S, 1024), F32),
        scratch_shapes=[pltpu.VMEM((DN_HEADS, LANES, LANES), F32)],
        compiler_params=_cp(("parallel", "arbitrary"), 32),
        name="gdn_scan_bwd" if reverse else "gdn_scan_fwd",
    )(qn, kn, vc, gates, gates_t)


def _gdn_out_kernel(of_ref, ob_ref, z_ref, nw_ref, y_ref):
    for h in range(DN_HEADS):
        sl = slice(h * LANES, (h + 1) * LANES)
        o = of_ref[:, sl] + ob_ref[:, sl]
        ms = jnp.mean(o * o, axis=-1, keepdims=True)
        on = o * lax.rsqrt(ms + EPS) * nw_ref[...]
        z = z_ref[:, sl].astype(F32)
        y_ref[:, sl] = (on * (z * jax.nn.sigmoid(z))).astype(y_ref.dtype)


def _gdn_out(o_f, o_b, proj, nw, base, tb=512):
    Tg = o_f.shape[0]
    return pl.pallas_call(
        _gdn_out_kernel,
        grid=(Tg // tb,),
        in_specs=[
            pl.BlockSpec((tb, 1024), lambda i: (i, 0)),
            pl.BlockSpec((tb, 1024), lambda i: (i, 0)),
            pl.BlockSpec((tb, 1024), lambda i: (base // tb + i, COL["cz"] // 1024)),
            pl.BlockSpec((1, LANES), lambda i: (0, 0)),
        ],
        out_specs=pl.BlockSpec((tb, 1024), lambda i: (i, 0)),
        out_shape=jax.ShapeDtypeStruct((Tg, 1024), BF16),
        compiler_params=_cp(("parallel",), 32),
        name="gdn_out",
    )(o_f, o_b, proj, nw)


def _outproj_kernel(ya_ref, yb_ref, yc_ref, x_ref, w_ref, nw_ref, *rest, with_router):
    dot = functools.partial(jnp.dot, preferred_element_type=F32)
    y = dot(ya_ref[...], w_ref[0:512, :]) + dot(yb_ref[...], w_ref[512:1024, :]) + dot(yc_ref[...], w_ref[1024:, :])
    x = x_ref[...] + y
    ms = jnp.mean(x * x, axis=-1, keepdims=True)
    h = x * lax.rsqrt(ms + EPS) * nw_ref[...]
    if with_router:
        wr_ref, xo_ref, h_ref, lg_ref = rest
        ha, hb, hc = _split3(h)
        wa, wb, wc = wr_ref[0], wr_ref[1], wr_ref[2]
        lg_ref[...] = (dot(ha, wa) + dot(ha, wb) + dot(hb, wa)) + (dot(ha, wc) + dot(hb, wb) + dot(hc, wa))
    else:
        xo_ref, h_ref = rest
    xo_ref[...] = x
    h_ref[...] = h.astype(h_ref.dtype)


def _outproj(ya, yb, yc, x, w, nw, wr3=None, tm=256):
    T = x.shape[0]
    with_router = wr3 is not None
    row = lambda width: pl.BlockSpec((tm, width), lambda i: (i, 0))
    in_specs = [row(512), row(512), row(1024), row(D_MODEL),
                pl.BlockSpec((D_MODEL, D_MODEL), lambda i: (0, 0)),
                pl.BlockSpec((1, D_MODEL), lambda i: (0, 0))]
    out_specs = [row(D_MODEL), row(D_MODEL)]
    out_shape = [jax.ShapeDtypeStruct((T, D_MODEL), F32), jax.ShapeDtypeStruct((T, D_MODEL), F32 if with_router else BF16)]
    args = [ya, yb, yc, x, w, nw]
    if with_router:
        in_specs.append(pl.BlockSpec((3, D_MODEL, LANES), lambda i: (0, 0, 0)))
        out_specs.append(row(LANES))
        out_shape.append(jax.ShapeDtypeStruct((T, LANES), F32))
        args.append(wr3)
    return pl.pallas_call(
        functools.partial(_outproj_kernel, with_router=with_router),
        grid=(T // tm,),
        in_specs=in_specs,
        out_specs=out_specs,
        out_shape=out_shape,
        compiler_params=_cp(("parallel",), 48),
        name="outproj_router" if with_router else "outproj",
    )(*args)


def _ffn_kernel(h_ref, x_ref, wg_ref, wu_ref, wd_ref, o_ref):
    f = pl.program_id(1)
    dot = functools.partial(jnp.dot, preferred_element_type=F32)
    h = h_ref[...]
    g = dot(h, wg_ref[...])
    u = dot(h, wu_ref[...])
    a = (g * jax.nn.sigmoid(g) * u).astype(BF16)
    dn = dot(a, wd_ref[...])

    @pl.when(f == 0)
    def _():
        o_ref[...] = x_ref[...] + dn

    @pl.when(f > 0)
    def _():
        o_ref[...] += dn


def _dense_ffn(h, x, wg, wu, wd, tm=512, tf=512):
    T = x.shape[0]
    F = wg.shape[1]
    return pl.pallas_call(
        _ffn_kernel,
        grid=(T // tm, F // tf),
        in_specs=[
            pl.BlockSpec((tm, D_MODEL), lambda i, f: (i, 0)),
            pl.BlockSpec((tm, D_MODEL), lambda i, f: (i, 0)),
            pl.BlockSpec((D_MODEL, tf), lambda i, f: (0, f)),
            pl.BlockSpec((D_MODEL, tf), lambda i, f: (0, f)),
            pl.BlockSpec((tf, D_MODEL), lambda i, f: (f, 0)),
        ],
        out_specs=pl.BlockSpec((tm, D_MODEL), lambda i, f: (i, 0)),
        out_shape=jax.ShapeDtypeStruct((T, D_MODEL), F32),
        compiler_params=_cp(("parallel", "arbitrary"), 48),
        name="dense_ffn",
    )(h, x, wg, wu, wd)


MOE_TM = 512
R_E1, R_E2, R_RANK1, R_RANK2, R_G1, R_G2 = 0, 1, 2, 3, 4, 5


def _route_kernel(lg_ref, r_ref, cnt_ref, carry_ref, *, tm):
    @pl.when(pl.program_id(0) == 0)
    def _():
        carry_ref[...] = jnp.zeros(carry_ref.shape, F32)

    lg = lg_ref[...]
    lane = lax.broadcasted_iota(jnp.int32, lg.shape, 1).astype(F32)
    l1 = jnp.where(lane < N_EXPERTS, lg, -jnp.inf)
    m1 = jnp.max(l1, axis=-1, keepdims=True)
    i1 = jnp.min(jnp.where(l1 == m1, lane, float(LANES)), axis=-1, keepdims=True)
    l2 = jnp.where(lane == i1, -jnp.inf, l1)
    m2 = jnp.max(l2, axis=-1, keepdims=True)
    i2 = jnp.min(jnp.where(l2 == m2, lane, float(LANES)), axis=-1, keepdims=True)
    e21 = jnp.exp(m2 - m1)
    g1 = 1.0 / (1.0 + e21)
    g2 = e21 / (1.0 + e21)
    oh1 = lane == i1
    oh2 = lane == i2
    oh = jnp.where(oh1 | oh2, 1.0, 0.0).astype(F32)
    r = lax.broadcasted_iota(jnp.int32, (tm, tm), 0)
    c = lax.broadcasted_iota(jnp.int32, (tm, tm), 1)
    tri = jnp.where(c < r, 1.0, 0.0).astype(BF16)
    before = jnp.dot(tri, oh.astype(BF16), preferred_element_type=F32) + carry_ref[...]
    rank1 = jnp.sum(jnp.where(oh1, before, 0.0), axis=-1, keepdims=True)
    rank2 = jnp.sum(jnp.where(oh2, before, 0.0), axis=-1, keepdims=True)
    carry_ref[...] += jnp.sum(oh, axis=0, keepdims=True)
    cnt_ref[...] = carry_ref[...]
    rec = jnp.where(lane == R_E1, i1, 0.0)
    rec = jnp.where(lane == R_E2, i2, rec)
    rec = jnp.where(lane == R_RANK1, rank1, rec)
    rec = jnp.where(lane == R_RANK2, rank2, rec)
    rec = jnp.where(lane == R_G1, g1, rec)
    rec = jnp.where(lane == R_G2, g2, rec)
    r_ref[...] = rec


def _route(logits, tm=512):
    T = logits.shape[0]
    return pl.pallas_call(
        functools.partial(_route_kernel, tm=tm),
        grid=(T // tm,),
        in_specs=[pl.BlockSpec((tm, LANES), lambda i: (i, 0))],
        out_specs=[pl.BlockSpec((tm, LANES), lambda i: (i, 0)), pl.BlockSpec((1, LANES), lambda i: (0, 0))],
        out_shape=[jax.ShapeDtypeStruct((T, LANES), F32), jax.ShapeDtypeStruct((1, LANES), F32)],
        scratch_shapes=[pltpu.VMEM((1, LANES), F32)],
        compiler_params=_cp(("arbitrary",), 32),
        name="moe_route",
    )(logits)


def _positions_kernel(r_ref, off_ref, p_ref):
    rec = r_ref[...]
    off = off_ref[...]
    lane = lax.broadcasted_iota(jnp.int32, rec.shape, 1)
    e1 = rec[:, R_E1:R_E1 + 1].astype(jnp.int32)
    e2 = rec[:, R_E2:R_E2 + 1].astype(jnp.int32)
    p1 = jnp.sum(jnp.where(lane == e1, off, 0.0), axis=-1, keepdims=True) + rec[:, R_RANK1:R_RANK1 + 1]
    p2 = jnp.sum(jnp.where(lane == e2, off, 0.0), axis=-1, keepdims=True) + rec[:, R_RANK2:R_RANK2 + 1]
    out = jnp.where(lane == 0, p1, jnp.where(lane == 1, p2, 0.0))
    p_ref[...] = out.astype(jnp.int32)


def _positions(rec, off_row, tm=512):
    T = rec.shape[0]
    return pl.pallas_call(
        _positions_kernel,
        grid=(T // tm,),
        in_specs=[pl.BlockSpec((tm, LANES), lambda i: (i, 0)), pl.BlockSpec((1, LANES), lambda i: (0, 0))],
        out_specs=pl.BlockSpec((tm, LANES), lambda i: (i, 0)),
        out_shape=jax.ShapeDtypeStruct((T, LANES), jnp.int32),
        compiler_params=_cp(("parallel",), 32),
        name="moe_positions",
    )(rec, off_row)


def _dispatch_kernel(pos_ref, h_ref, xs_in_ref, xs_ref, sem, *, tm):
    del xs_in_ref
    i = pl.program_id(0)

    def row_copy(r, s):
        src = h_ref.at[pl.ds(i * tm + r, 1), :]
        dst = xs_ref.at[pl.ds(pos_ref[2 * r + s], 1), :]
        return pltpu.make_async_copy(src, dst, sem)

    def start(r, c):
        row_copy(r, 0).start()
        row_copy(r, 1).start()
        return c

    def wait(r, c):
        row_copy(r, 0).wait()
        row_copy(r, 1).wait()
        return c

    lax.fori_loop(0, tm, start, 0)
    lax.fori_loop(0, tm, wait, 0)


def _dispatch(pos_flat, h, xs_zero, tm=512):
    T = h.shape[0]
    return pl.pallas_call(
        functools.partial(_dispatch_kernel, tm=tm),
        grid=(T // tm,),
        in_specs=[
            pl.BlockSpec((2 * tm,), lambda i: (i,), memory_space=pltpu.SMEM),
            pl.BlockSpec(memory_space=pl.ANY),
            pl.BlockSpec(memory_space=pl.ANY),
        ],
        out_specs=pl.BlockSpec(memory_space=pl.ANY),
        out_shape=jax.ShapeDtypeStruct(xs_zero.shape, xs_zero.dtype),
        scratch_shapes=[pltpu.SemaphoreType.DMA(())],
        input_output_aliases={2: 0},
        compiler_params=_cp(("arbitrary",), 32),
        name="moe_dispatch",
    )(pos_flat, h, xs_zero)


def _expert_kernel(te_ref, na_ref, x_ref, wg_ref, wu_ref, wd_ref, o_ref, hb_ref):
    i = pl.program_id(0)
    f = pl.program_id(1)
    dot = functools.partial(jnp.dot, preferred_element_type=F32)

    @pl.when(i < na_ref[0])
    def _():
        @pl.when(f == 0)
        def _():
            hb_ref[...] = x_ref[...].astype(BF16)

        h = hb_ref[...]
        g = dot(h, wg_ref[0])
        u = dot(h, wu_ref[0])
        a = (g * jax.nn.sigmoid(g) * u).astype(BF16)
        dn = dot(a, wd_ref[0])

        @pl.when(f == 0)
        def _():
            o_ref[...] = dn

        @pl.when(f > 0)
        def _():
            o_ref[...] += dn

    @pl.when((i >= na_ref[0]) & (f == 0))
    def _():
        o_ref[...] = jnp.zeros(o_ref.shape, F32)


def _expert_ffn(tile_expert, n_active, xs, wg, wu, wd, tm=MOE_TM, tf=512):
    n_tiles = xs.shape[0] // tm
    F = wg.shape[2]
    nf = F // tf

    def row(i, f, te, na):
        return (jnp.minimum(i, na[0] - 1), 0)

    def fcol(i, f, na):
        return jnp.where(i < na[0], f, nf - 1)

    def expert(i, te, na):
        return te[jnp.minimum(i, na[0] - 1)]

    grid_spec = pltpu.PrefetchScalarGridSpec(
        num_scalar_prefetch=2,
        grid=(n_tiles, nf),
        in_specs=[
            pl.BlockSpec((tm, D_MODEL), row),
            pl.BlockSpec((1, D_MODEL, tf), lambda i, f, te, na: (expert(i, te, na), 0, fcol(i, f, na))),
            pl.BlockSpec((1, D_MODEL, tf), lambda i, f, te, na: (expert(i, te, na), 0, fcol(i, f, na))),
            pl.BlockSpec((1, tf, D_MODEL), lambda i, f, te, na: (expert(i, te, na), fcol(i, f, na), 0)),
        ],
        out_specs=pl.BlockSpec((tm, D_MODEL), lambda i, f, te, na: (i, 0)),
        scratch_shapes=[pltpu.VMEM((tm, D_MODEL), BF16)],
    )
    return pl.pallas_call(
        _expert_kernel,
        grid_spec=grid_spec,
        out_shape=jax.ShapeDtypeStruct(xs.shape, F32),
        compiler_params=_cp(("arbitrary", "arbitrary"), 48),
        name="moe_experts",
    )(tile_expert, n_active, xs, wg, wu, wd)


def _combine_kernel(pos_ref, ys_ref, x_ref, r_ref, nw_ref, o_ref, y1_ref, y2_ref, sem, *, tm, final_norm):
    def row_copy(r, s):
        dst = (y1_ref, y2_ref)[s].at[pl.ds(r, 1), :]
        src = ys_ref.at[pl.ds(pos_ref[2 * r + s], 1), :]
        return pltpu.make_async_copy(src, dst, sem)

    def start(r, c):
        row_copy(r, 0).start()
        row_copy(r, 1).start()
        return c

    def wait(r, c):
        row_copy(r, 0).wait()
        row_copy(r, 1).wait()
        return c

    lax.fori_loop(0, tm, start, 0)
    lax.fori_loop(0, tm, wait, 0)
    rec = r_ref[...]
    g1 = rec[:, R_G1:R_G1 + 1]
    g2 = rec[:, R_G2:R_G2 + 1]
    x = x_ref[...] + (g1 * y1_ref[...] + g2 * y2_ref[...])
    if final_norm:
        ms = jnp.mean(x * x, axis=-1, keepdims=True)
        x = x * lax.rsqrt(ms + EPS) * nw_ref[...]
    o_ref[...] = x


def _combine(pos_flat, ys, x, rec, nw, base, n_rows, final_norm, tm=256):
    rb = base // tm
    return pl.pallas_call(
        functools.partial(_combine_kernel, tm=tm, final_norm=final_norm),
        grid=(n_rows // tm,),
        in_specs=[
            pl.BlockSpec((2 * tm,), lambda i: (rb + i,), memory_space=pltpu.SMEM),
            pl.BlockSpec(memory_space=pl.ANY),
            pl.BlockSpec((tm, D_MODEL), lambda i: (rb + i, 0)),
            pl.BlockSpec((tm, LANES), lambda i: (rb + i, 0)),
            pl.BlockSpec((1, D_MODEL), lambda i: (0, 0)),
        ],
        out_specs=pl.BlockSpec((tm, D_MODEL), lambda i: (i, 0)),
        out_shape=jax.ShapeDtypeStruct((n_rows, D_MODEL), F32),
        scratch_shapes=[
            pltpu.VMEM((tm, D_MODEL), F32),
            pltpu.VMEM((tm, D_MODEL), F32),
            pltpu.SemaphoreType.DMA(()),
        ],
        compiler_params=_cp(("arbitrary",), 32),
        name="moe_combine",
    )(pos_flat, ys, x, rec, nw)


def _permute_w_in(w):
    starts = np.cumsum((0,) + REF_SPLITS)
    seg = {n: w[:, starts[i]:starts[i + 1]] for i, n in enumerate(REF_NAMES)}
    main = jnp.concatenate([seg[n] for n in MY_ORDER], axis=1).astype(BF16)
    cab = jnp.pad(seg["cab"], ((0, 0), (0, LANES - 32))).astype(BF16)
    return main, cab


def _lane_row(v):
    v = v.reshape(1, -1).astype(F32)
    return jnp.pad(v, ((0, 0), (0, LANES - v.shape[1])))


def _mixers(proj, cab, groups, p):
    ya, yb, yc = [], [], []
    bias = _na_bias_tables(p["rpb"])
    for base, B, S in groups:
        cos, sin = _rope_tables(S)
        qa, ka = _attn_prep(proj, cos, sin, p["q_norm"], p["k_norm"], base, B, S)
        ya.append(_flash(qa, ka, proj, base, B, S))
        yb.append(_na(proj, bias, base, B, S))
        qn, kn, vc, gates = _gdn_prep(proj, cab, p["conv_w"], p["nega"], p["dtb"], base, B, S)
        gates_t = gates.reshape(B * S // CHUNK, CHUNK, LANES).transpose(0, 2, 1)
        o_f = _gdn_scan(qn, kn, vc, gates, gates_t, B, S, reverse=False)
        o_b = _gdn_scan(qn, kn, vc, gates, gates_t, B, S, reverse=True)
        yc.append(_gdn_out(o_f, o_b, proj, p["out_norm"], base))
    return jnp.concatenate(ya), jnp.concatenate(yb), jnp.concatenate(yc)


def _moe(h, x, logits, wg, wu, wd, final_nw, groups):
    T = h.shape[0]
    rec, cnt = _route(logits)
    counts = cnt[0, :N_EXPERTS].astype(jnp.int32)
    padded = ((counts + MOE_TM - 1) // MOE_TM) * MOE_TM
    ends = jnp.cumsum(padded)
    off = ends - padded
    n_rows = 2 * T + N_EXPERTS * MOE_TM
    n_tiles = n_rows // MOE_TM
    n_active = (ends[-1] // MOE_TM).astype(jnp.int32).reshape(1)
    tile_start = jnp.arange(n_tiles, dtype=jnp.int32) * MOE_TM
    tile_expert = jnp.minimum(jnp.sum(tile_start[:, None] >= ends[None, :], axis=1), N_EXPERTS - 1).astype(jnp.int32)
    off_row = _lane_row(off.astype(F32))
    pos = _positions(rec, off_row)[:, :2].reshape(-1)
    xs = _dispatch(pos, h, jnp.zeros((n_rows, D_MODEL), F32))
    ys = _expert_ffn(tile_expert, n_active, xs, wg, wu, wd)
    outs = []
    for base, B, S in groups:
        y = _combine(pos, ys, x, rec, final_nw, base, B * S, final_norm=True)
        outs.append(y.reshape(B, S, D_MODEL))
    return outs


def kernel(x_prompt, x_sample, mix_norm, w_in, q_norm, k_norm, rpb, conv_w, a_log, dt_bias, out_norm, w_out,
           ffn_norm, dense_gate, dense_up, dense_down, router, moe_gate, moe_up, moe_down, final_norm):
    depth = mix_norm.shape[0]
    assert depth == 2, "layer 0 dense FFN, layer 1 MoE FFN followed by the final norm"
    Bp, Sp, _ = x_prompt.shape
    Bs, Ss, _ = x_sample.shape
    groups = ((0, Bp, Sp), (Bp * Sp, Bs, Ss))
    x = jnp.concatenate([x_prompt.reshape(Bp * Sp, D_MODEL), x_sample.reshape(Bs * Ss, D_MODEL)], axis=0)
    outs = None
    for l in range(depth):
        w_main, w_cab = _permute_w_in(w_in[l])
        p = {
            "q_norm": _lane_row(q_norm[l]), "k_norm": _lane_row(k_norm[l]), "rpb": rpb[l],
            "conv_w": conv_w[l].astype(F32),
            "nega": _lane_row(-jnp.exp(a_log[l].astype(F32))), "dtb": _lane_row(dt_bias[l]),
            "out_norm": _lane_row(out_norm[l]),
        }
        proj, cab = _inproj(x, mix_norm[l].reshape(1, D_MODEL), w_main, w_cab)
        ya, yb, yc = _mixers(proj, cab, groups, p)
        wo = w_out[l].astype(BF16)
        fnw = ffn_norm[l].reshape(1, D_MODEL)
        if l % 2 == 0:
            x, h = _outproj(ya, yb, yc, x, wo, fnw)
            x = _dense_ffn(h, x, dense_gate[l // 2].astype(BF16), dense_up[l // 2].astype(BF16),
                           dense_down[l // 2].astype(BF16))
        else:
            wr = jnp.pad(router[l // 2].astype(F32), ((0, 0), (0, LANES - N_EXPERTS)))
            x, h, logits = _outproj(ya, yb, yc, x, wo, fnw, jnp.stack(_split3(wr)))
            outs = _moe(h, x, logits, moe_gate[l // 2].astype(BF16), moe_up[l // 2].astype(BF16),
                        moe_down[l // 2].astype(BF16), final_norm.reshape(1, D_MODEL), groups)
    return tuple(outs)
```

```python
import functools

import numpy as np
import jax
import jax.numpy as jnp
from jax import lax
from jax.experimental import pallas as pl
from jax.experimental.pallas import tpu as pltpu

F32 = jnp.float32
BF16 = jnp.bfloat16
EPS = 1e-6

D_MODEL = 2048
GRID_W = 64
HEAD_DIM = 128
ATT_HEADS = 4
ATT_KV_HEADS = 2
ROPE_THETA = 10000.0
NA_HEADS = 4
NA_WIN_ROWS = 8
NA_WIN_COLS = 16
DN_HEADS = 8
CONV_K = 5
CHUNK = 64
N_EXPERTS = 8
LANES = 128

PROJ_MAIN = 6656
REF_SPLITS = (512, 256, 256, 512, 512, 512, 1024, 1024, 1024, 1024, 32)
REF_NAMES = ("aq", "ak", "av", "bq", "bk", "bv", "cq", "ck", "cv", "cz", "cab")
MY_ORDER = ("cq", "ck", "cv", "cz", "aq", "bq", "bk", "bv", "ak", "av")
COL = {}
_off = 0
for _n in MY_ORDER:
    COL[_n] = _off
    _off += REF_SPLITS[REF_NAMES.index(_n)]
assert _off == PROJ_MAIN

MIB = 1024 * 1024


def _cp(sem, vmem_mib):
    return pltpu.CompilerParams(dimension_semantics=sem, vmem_limit_bytes=vmem_mib * MIB)


def _inproj_kernel(x_ref, nw_ref, w_ref, wc_ref, o_ref, oc_ref, h_ref):
    @pl.when(pl.program_id(1) == 0)
    def _():
        x = x_ref[...]
        ms = jnp.mean(x * x, axis=-1, keepdims=True)
        hb = (x * lax.rsqrt(ms + EPS) * nw_ref[...]).astype(BF16)
        h_ref[...] = hb
        oc_ref[...] = jnp.dot(hb, wc_ref[...], preferred_element_type=F32)

    o_ref[...] = jnp.dot(h_ref[...], w_ref[...], preferred_element_type=F32).astype(o_ref.dtype)


def _inproj(x, nw, w, wc, tm=1024, tn=512):
    T = x.shape[0]
    return pl.pallas_call(
        _inproj_kernel,
        grid=(T // tm, PROJ_MAIN // tn),
        in_specs=[
            pl.BlockSpec((tm, D_MODEL), lambda i, j: (i, 0)),
            pl.BlockSpec((1, D_MODEL), lambda i, j: (0, 0)),
            pl.BlockSpec((D_MODEL, tn), lambda i, j: (0, j)),
            pl.BlockSpec((D_MODEL, LANES), lambda i, j: (0, 0)),
        ],
        out_specs=[
            pl.BlockSpec((tm, tn), lambda i, j: (i, j)),
            pl.BlockSpec((tm, LANES), lambda i, j: (i, 0)),
        ],
        out_shape=[jax.ShapeDtypeStruct((T, PROJ_MAIN), BF16), jax.ShapeDtypeStruct((T, LANES), F32)],
        scratch_shapes=[pltpu.VMEM((tm, D_MODEL), BF16)],
        compiler_params=_cp(("parallel", "arbitrary"), 40),
        name="inproj",
    )(x, nw, w, wc)


def _rope_tables(S):
    half = HEAD_DIM // 2
    quarter = half // 2
    t = jnp.arange(S)
    inv = ROPE_THETA ** (-jnp.arange(quarter, dtype=F32) / quarter)
    ang_r = (t // GRID_W).astype(F32)[:, None] * inv
    ang_c = (t % GRID_W).astype(F32)[:, None] * inv
    cos = jnp.concatenate([jnp.cos(ang_r), jnp.cos(ang_r), jnp.cos(ang_c), jnp.cos(ang_c)], axis=-1)
    sin = jnp.concatenate([-jnp.sin(ang_r), jnp.sin(ang_r), -jnp.sin(ang_c), jnp.sin(ang_c)], axis=-1)
    return cos, sin


def _norm_rope(xh, nw, cos, sin, first_quarter):
    ms = jnp.mean(xh * xh, axis=-1, keepdims=True)
    xn = xh * lax.rsqrt(ms + EPS) * nw
    partner = jnp.where(first_quarter, pltpu.roll(xn, 96, 1), pltpu.roll(xn, 32, 1))
    return xn * cos + partner * sin


def _attn_prep_kernel(q_ref, k_ref, cos_ref, sin_ref, qn_ref, kn_ref, qo_ref, ko_ref):
    cos = cos_ref[...]
    sin = sin_ref[...]
    lane = lax.broadcasted_iota(jnp.int32, cos.shape, 1)
    first_quarter = (lane % 64) < 32
    scale = HEAD_DIM ** -0.5
    for h in range(ATT_HEADS):
        sl = slice(h * HEAD_DIM, (h + 1) * HEAD_DIM)
        y = _norm_rope(q_ref[:, sl].astype(F32), qn_ref[...], cos, sin, first_quarter)
        qo_ref[:, sl] = (y * scale).astype(BF16)
    for h in range(ATT_KV_HEADS):
        sl = slice(h * HEAD_DIM, (h + 1) * HEAD_DIM)
        y = _norm_rope(k_ref[:, sl].astype(F32), kn_ref[...], cos, sin, first_quarter)
        ko_ref[:, sl] = y.astype(BF16)


def _attn_prep(proj, cos, sin, qn, kn, base, B, S, tm=256):
    nb = S // tm
    rb = base // tm
    return pl.pallas_call(
        _attn_prep_kernel,
        grid=(B * nb,),
        in_specs=[
            pl.BlockSpec((tm, 512), lambda i: (rb + i, COL["aq"] // 512)),
            pl.BlockSpec((tm, 256), lambda i: (rb + i, COL["ak"] // 256)),
            pl.BlockSpec((tm, LANES), lambda i: (i % nb, 0)),
            pl.BlockSpec((tm, LANES), lambda i: (i % nb, 0)),
            pl.BlockSpec((1, LANES), lambda i: (0, 0)),
            pl.BlockSpec((1, LANES), lambda i: (0, 0)),
        ],
        out_specs=[
            pl.BlockSpec((tm, 512), lambda i: (i, 0)),
            pl.BlockSpec((tm, 256), lambda i: (i, 0)),
        ],
        out_shape=[jax.ShapeDtypeStruct((B * S, 512), BF16), jax.ShapeDtypeStruct((B * S, 256), BF16)],
        compiler_params=_cp(("parallel",), 32),
        name="attn_prep",
    )(proj, proj, cos, sin, qn, kn)


def _flash_kernel(q_ref, k_ref, v_ref, o_ref, m_ref, l_ref, acc_ref, *, tq, tk, S):
    q = q_ref[...]
    q2 = jnp.concatenate([q[:, :HEAD_DIM], q[:, HEAD_DIM:]], axis=0)
    m_ref[...] = jnp.full(m_ref.shape, -jnp.inf, F32)
    l_ref[...] = jnp.zeros(l_ref.shape, F32)
    acc_ref[...] = jnp.zeros(acc_ref.shape, F32)

    def body(j, carry):
        start = pl.multiple_of(j * tk, tk)
        k = k_ref[pl.ds(start, tk), :]
        v = v_ref[pl.ds(start, tk), :]
        s = lax.dot_general(q2, k, (((1,), (1,)), ((), ())), preferred_element_type=F32)
        m_prev = m_ref[...]
        m_new = jnp.maximum(m_prev, jnp.max(s, axis=-1, keepdims=True))
        alpha = jnp.exp(m_prev - m_new)
        p = jnp.exp(s - m_new[:, :1])
        l_ref[...] = alpha * l_ref[...] + jnp.sum(p, axis=-1, keepdims=True)
        acc_ref[...] = alpha * acc_ref[...] + jnp.dot(p.astype(BF16), v, preferred_element_type=F32)
        m_ref[...] = m_new
        return carry

    lax.fori_loop(0, S // tk, body, 0)
    o = acc_ref[...] / l_ref[...]
    o_ref[...] = jnp.concatenate([o[:tq], o[tq:]], axis=1).astype(o_ref.dtype)


def _flash(qa, ka, proj, base, B, S, tq=512, tk=512):
    nq = S // tq
    vb = COL["av"] // HEAD_DIM
    return pl.pallas_call(
        functools.partial(_flash_kernel, tq=tq, tk=tk, S=S),
        grid=(B, ATT_KV_HEADS, nq),
        in_specs=[
            pl.BlockSpec((tq, 256), lambda b, h, i: (b * nq + i, h)),
            pl.BlockSpec((S, HEAD_DIM), lambda b, h, i: (b, h)),
            pl.BlockSpec((S, HEAD_DIM), lambda b, h, i: (base // S + b, vb + h)),
        ],
        out_specs=pl.BlockSpec((tq, 256), lambda b, h, i: (b * nq + i, h)),
        out_shape=jax.ShapeDtypeStruct((B * S, 512), BF16),
        scratch_shapes=[
            pltpu.VMEM((2 * tq, HEAD_DIM), F32),
            pltpu.VMEM((2 * tq, HEAD_DIM), F32),
            pltpu.VMEM((2 * tq, HEAD_DIM), F32),
        ],
        compiler_params=_cp(("parallel", "parallel", "arbitrary"), 32),
        name="flash_gqa",
    )(qa, ka, proj)


NA_QROWS = 8
NA_KROWS = 16
NA_TQ = NA_QROWS * GRID_W
NA_TK = NA_KROWS * GRID_W
NA_NEG = -1e30


def _na_bias_tables(rpb):
    qr = np.arange(NA_QROWS)[:, None]
    kr = np.arange(NA_KROWS)[None, :]
    c = np.arange(GRID_W)[:, None]
    kc = np.arange(GRID_W)[None, :]
    cs = np.clip(c - NA_WIN_COLS // 2, 0, GRID_W - NA_WIN_COLS)
    col_ok = (kc >= cs) & (kc < cs + NA_WIN_COLS)
    dc = np.clip(kc - c, -(NA_WIN_COLS - 1), NA_WIN_COLS - 1) + NA_WIN_COLS - 1
    col_sel = (dc[..., None] == np.arange(2 * NA_WIN_COLS - 1)).astype(np.float32)
    by_col = jnp.einsum("hab,cmb->hacm", rpb.astype(F32), col_sel, precision=lax.Precision.HIGHEST)
    tables = []
    for off, lo in ((0, np.maximum(qr - 4, 0)), (-4, qr + 0), (-8, 8 + np.minimum(qr - 4, 0))):
        row_ok = (kr >= lo) & (kr < lo + NA_WIN_ROWS)
        dr = np.clip(off + kr - qr + NA_WIN_ROWS - 1, 0, 2 * NA_WIN_ROWS - 2)
        row_sel = (dr[..., None] == np.arange(2 * NA_WIN_ROWS - 1)).astype(np.float32)
        bias = jnp.einsum("qka,hacm->hqckm", row_sel, by_col, precision=lax.Precision.HIGHEST)
        ok = row_ok[:, None, :, None] & col_ok[None, :, None, :]
        tables.append(jnp.where(jnp.asarray(ok)[None], bias, NA_NEG).reshape(NA_HEADS, NA_TQ, NA_TK))
    return jnp.stack(tables)


def _na_kernel(q_ref, k_ref, v_ref, b_ref, o_ref, *, S):
    j = pl.program_id(2)
    start = pl.multiple_of(jnp.clip(j * NA_TQ - 4 * GRID_W, 0, S - NA_TK), 4 * GRID_W)
    k = k_ref[pl.ds(start, NA_TK), :]
    v = v_ref[pl.ds(start, NA_TK), :]
    s = lax.dot_general(q_ref[...], k, (((1,), (1,)), ((), ())), preferred_element_type=F32)
    s = s * (HEAD_DIM ** -0.5) + b_ref[0, 0]
    m = jnp.max(s, axis=-1, keepdims=True)
    p = jnp.exp(s - m)
    l = jnp.sum(p, axis=-1, keepdims=True)
    o = jnp.dot(p.astype(BF16), v, preferred_element_type=F32)
    o_ref[...] = (o / l).astype(o_ref.dtype)


def _na(proj, bias, base, B, S):
    assert S % NA_TQ == 0 and S >= NA_TK
    nj = S // NA_TQ
    qb, kb, vb = COL["bq"] // HEAD_DIM, COL["bk"] // HEAD_DIM, COL["bv"] // HEAD_DIM

    def case(j):
        return jnp.where(j == 0, 0, jnp.where(j == nj - 1, 2, 1))

    return pl.pallas_call(
        functools.partial(_na_kernel, S=S),
        grid=(B, NA_HEADS, nj),
        in_specs=[
            pl.BlockSpec((NA_TQ, HEAD_DIM), lambda b, h, j: (base // NA_TQ + b * nj + j, qb + h)),
            pl.BlockSpec((S, HEAD_DIM), lambda b, h, j: (base // S + b, kb + h)),
            pl.BlockSpec((S, HEAD_DIM), lambda b, h, j: (base // S + b, vb + h)),
            pl.BlockSpec((1, 1, NA_TQ, NA_TK), lambda b, h, j: (case(j), h, 0, 0)),
        ],
        out_specs=pl.BlockSpec((NA_TQ, HEAD_DIM), lambda b, h, j: (b * nj + j, h)),
        out_shape=jax.ShapeDtypeStruct((B * S, NA_HEADS * HEAD_DIM), BF16),
        compiler_params=_cp(("parallel", "parallel", "arbitrary"), 40),
        name="nbr_attn",
    )(proj, proj, proj, bias)


G_GC, G_BETA, G_EGC, G_EKD, G_ELAST, G_BEXP = 0, 16, 32, 48, 64, 80
GDN_TB = 256
HALO = 16


def _split3(x):
    a = x.astype(BF16)
    r = x - a.astype(F32)
    b = r.astype(BF16)
    c = (r - b.astype(F32)).astype(BF16)
    return a, b, c


def _dot_exact_lhs(m_bf16, x):
    a, b, c = _split3(x)
    d = functools.partial(jnp.dot, preferred_element_type=F32)
    return d(m_bf16, a) + d(m_bf16, b) + d(m_bf16, c)


def _gdn_prep_kernel(x_ref, xp_ref, xn_ref, cw_ref, cab_ref, nega_ref, dtb_ref,
                     q_ref, k_ref, v_ref, g_ref, ext_ref, *, nb):
    i = pl.program_id(0)
    tb = GDN_TB
    first = (i % nb) == 0
    last = (i % nb) == nb - 1
    ext_ref[0:HALO, :] = jnp.where(first, 0.0, xp_ref[...].astype(F32))
    ext_ref[HALO:HALO + tb, :] = x_ref[...].astype(F32)
    ext_ref[HALO + tb:, :] = jnp.where(last, 0.0, xn_ref[...].astype(F32))
    outs = (q_ref, k_ref, v_ref)
    for c in range(3 * DN_HEADS):
        sl = slice(c * LANES, (c + 1) * LANES)
        acc = jnp.zeros((tb, LANES), F32)
        for t in range(CONV_K):
            acc = acc + cw_ref[t:t + 1, sl] * ext_ref[HALO - CONV_K // 2 + t:HALO - CONV_K // 2 + t + tb, sl]
        y = acc * jax.nn.sigmoid(acc)
        which, h = divmod(c, DN_HEADS)
        if which < 2:
            y = y * lax.rsqrt(jnp.sum(y * y, axis=-1, keepdims=True) + EPS)
        if which == 0:
            y = y * (HEAD_DIM ** -0.5)
        outs[which][:, h * LANES:(h + 1) * LANES] = y.astype(BF16)

    cab = cab_ref[...]
    lane = lax.broadcasted_iota(jnp.int32, cab.shape, 1)
    z = cab + dtb_ref[...]
    softplus = jnp.maximum(z, 0.0) + jnp.log1p(jnp.exp(-jnp.abs(z)))
    g = jnp.where(lane < 16, nega_ref[...] * softplus, 0.0)
    beta = jnp.where((lane >= 16) & (lane < 32), jax.nn.sigmoid(cab), 0.0)
    r = lax.broadcasted_iota(jnp.int32, (tb, tb), 0)
    cidx = lax.broadcasted_iota(jnp.int32, (tb, tb), 1)
    same = (r // CHUNK) == (cidx // CHUNK)
    allm32 = jnp.where(same, 1.0, 0.0).astype(F32)
    low = jnp.where(cidx <= r, allm32, 0.0).astype(BF16)
    upp = jnp.where(cidx >= r, allm32, 0.0).astype(BF16)
    allm = allm32.astype(BF16)
    gc = jnp.where(lane < 8, _dot_exact_lhs(low, g), _dot_exact_lhs(upp, g))
    tot = _dot_exact_lhs(allm, g)
    egc = jnp.exp(gc)
    out = gc
    out = out + beta
    out = out + pltpu.roll(jnp.where(lane < 16, egc, 0.0), G_EGC, 1)
    out = out + pltpu.roll(jnp.where(lane < 16, jnp.exp(tot - gc), 0.0), G_EKD, 1)
    out = out + pltpu.roll(jnp.where(lane < 16, jnp.exp(tot), 0.0), G_ELAST, 1)
    bexp = beta * pltpu.roll(jnp.where(lane < 16, egc, 0.0), 16, 1)
    out = out + pltpu.roll(bexp, G_BEXP - 16, 1)
    g_ref[...] = out


def _gdn_prep(proj, cab, conv_w, nega, dtb, base, B, S):
    tb = GDN_TB
    nb = S // tb
    rb = base // tb
    hb = tb // HALO
    nrows = proj.shape[0] // HALO
    width = 3 * DN_HEADS * LANES
    return pl.pallas_call(
        functools.partial(_gdn_prep_kernel, nb=nb),
        grid=(B * nb,),
        in_specs=[
            pl.BlockSpec((tb, width), lambda i: (rb + i, 0)),
            pl.BlockSpec((HALO, width), lambda i: (jnp.maximum((rb + i) * hb - 1, 0), 0)),
            pl.BlockSpec((HALO, width), lambda i: (jnp.minimum((rb + i + 1) * hb, nrows - 1), 0)),
            pl.BlockSpec((CONV_K, width), lambda i: (0, 0)),
            pl.BlockSpec((tb, LANES), lambda i: (rb + i, 0)),
            pl.BlockSpec((1, LANES), lambda i: (0, 0)),
            pl.BlockSpec((1, LANES), lambda i: (0, 0)),
        ],
        out_specs=[
            pl.BlockSpec((tb, 1024), lambda i: (i, 0)),
            pl.BlockSpec((tb, 1024), lambda i: (i, 0)),
            pl.BlockSpec((tb, 1024), lambda i: (i, 0)),
            pl.BlockSpec((tb, LANES), lambda i: (i, 0)),
        ],
        out_shape=[jax.ShapeDtypeStruct((B * S, 1024), BF16)] * 3 + [jax.ShapeDtypeStruct((B * S, LANES), F32)],
        scratch_shapes=[pltpu.VMEM((tb + 2 * HALO, width), F32)],
        compiler_params=_cp(("parallel",), 40),
        name="gdn_prep",
    )(proj, proj, proj, conv_w, cab, nega, dtb)


def _gdn_scan_kernel(qf_ref, kf_ref, vf_ref, gf_ref, gtf_ref, qb_ref, kb_ref, vb_ref, gb_ref, gtb_ref,
                     of_ref, ob_ref, s_ref):
    @pl.when(pl.program_id(1) == 0)
    def _():
        s_ref[...] = jnp.zeros(s_ref.shape, F32)

    C = CHUNK
    refs = ((qf_ref, kf_ref, vf_ref, gf_ref, gtf_ref, of_ref), (qb_ref, kb_ref, vb_ref, gb_ref, gtb_ref, ob_ref))
    chains = [(d, h) for h in range(DN_HEADS) for d in (0, 1)]
    ri = lax.broadcasted_iota(jnp.int32, (C, C), 0)
    ci = lax.broadcasted_iota(jnp.int32, (C, C), 1)
    incl = (ri >= ci, ri <= ci)
    strict = (ri > ci, ri < ci)
    eye = jnp.where(ri == ci, 1.0, 0.0).astype(F32)
    dot = functools.partial(jnp.dot, preferred_element_type=F32)
    gcol = (gf_ref[...], gb_ref[...])
    grow = (gtf_ref[0], gtb_ref[0])

    def colb(field, ch, width):
        d, h = ch
        idx = field + d * DN_HEADS + h
        return jnp.broadcast_to(gcol[d][:, idx:idx + 1], (C, width))

    def rowb(field, ch):
        d, h = ch
        idx = field + d * DN_HEADS + h
        return jnp.broadcast_to(grow[d][idx:idx + 1, :], (C, C))

    def sl(ch):
        return slice(ch[1] * LANES, (ch[1] + 1) * LANES)

    def sidx(ch):
        return ch[0] * DN_HEADS + ch[1]

    q = {ch: refs[ch[0]][0][:, sl(ch)] for ch in chains}
    k = {ch: refs[ch[0]][1][:, sl(ch)] for ch in chains}
    v = {ch: refs[ch[0]][2][:, sl(ch)] for ch in chains}
    kq = {ch: lax.dot_general(jnp.concatenate([k[ch], q[ch]], axis=0), k[ch], (((1,), (1,)), ((), ())),
                              preferred_element_type=F32) for ch in chains}
    sb = {ch: s_ref[sidx(ch)].astype(BF16) for ch in chains}
    qs = {ch: dot(q[ch], sb[ch]) for ch in chains}
    decay = {ch: jnp.where(incl[ch[0]], jnp.exp(colb(G_GC, ch, C) - rowb(G_GC, ch)), 0.0) for ch in chains}
    n = {ch: jnp.where(strict[ch[0]], -(kq[ch][:C] * colb(G_BETA, ch, C) * decay[ch]), 0.0) for ch in chains}
    qk = {ch: (kq[ch][C:] * decay[ch]).astype(BF16) for ch in chains}
    x = {ch: eye + n[ch] for ch in chains}
    nb = {ch: n[ch].astype(BF16) for ch in chains}
    p = {ch: dot(nb[ch], nb[ch]) for ch in chains}
    steps = int(np.log2(C)) - 1
    for it in range(steps):
        pb = {ch: p[ch].astype(BF16) for ch in chains}
        if it < steps - 1:
            xp = {ch: dot(pb[ch], jnp.concatenate([x[ch].astype(BF16), pb[ch]], axis=1)) for ch in chains}
            x = {ch: x[ch] + xp[ch][:, :C] for ch in chains}
            p = {ch: xp[ch][:, C:] for ch in chains}
        else:
            x = {ch: x[ch] + dot(pb[ch], x[ch].astype(BF16)) for ch in chains}
    u = {ch: dot((x[ch] * rowb(G_BETA, ch)).astype(BF16), v[ch]) for ch in chains}
    w = {ch: dot((x[ch] * rowb(G_BEXP, ch)).astype(BF16), k[ch]).astype(BF16) for ch in chains}
    kd = {ch: (k[ch].astype(F32) * colb(G_EKD, ch, LANES)).T.astype(BF16) for ch in chains}
    v_new = {ch: (u[ch] - dot(w[ch], sb[ch])).astype(BF16) for ch in chains}
    for ch in chains:
        refs[ch[0]][5][:, sl(ch)] = colb(G_EGC, ch, LANES) * qs[ch] + dot(qk[ch], v_new[ch])
    for ch in chains:
        d, h = ch
        idx = G_ELAST + d * DN_HEADS + h
        glast = jnp.broadcast_to(gcol[d][0:1, idx:idx + 1], (LANES, LANES))
        s_ref[sidx(ch)] = s_ref[sidx(ch)] * glast + dot(kd[ch], v_new[ch])


def _gdn_scan(qn, kn, vc, gates, gates_t, B, S):
    n = S // CHUNK
    fwd = lambda b, c: b * n + c
    bwd = lambda b, c: b * n + (n - 1 - c)

    def specs(idx):
        return [
            pl.BlockSpec((CHUNK, 1024), lambda b, c: (idx(b, c), 0)),
            pl.BlockSpec((CHUNK, 1024), lambda b, c: (idx(b, c), 0)),
            pl.BlockSpec((CHUNK, 1024), lambda b, c: (idx(b, c), 0)),
            pl.BlockSpec((CHUNK, LANES), lambda b, c: (idx(b, c), 0)),
            pl.BlockSpec((1, LANES, CHUNK), lambda b, c: (idx(b, c), 0, 0)),
        ]

    return pl.pallas_call(
        _gdn_scan_kernel,
        grid=(B, n),
        in_specs=specs(fwd) + specs(bwd),
        out_specs=[pl.BlockSpec((CHUNK, 1024), lambda b, c: (fwd(b, c), 0)),
                   pl.BlockSpec((CHUNK, 1024), lambda b, c: (bwd(b, c), 0))],
        out_shape=[jax.ShapeDtypeStruct((B * S, 1024), F32)] * 2,
        scratch_shapes=[pltpu.VMEM((2 * DN_HEADS, LANES, LANES), F32)],
        compiler_params=_cp(("parallel", "arbitrary"), 32),
        name="gdn_scan",
    )(qn, kn, vc, gates, gates_t, qn, kn, vc, gates, gates_t)


def _gdn_out_kernel(of_ref, ob_ref, z_ref, nw_ref, y_ref):
    for h in range(DN_HEADS):
        sl = slice(h * LANES, (h + 1) * LANES)
        o = of_ref[:, sl] + ob_ref[:, sl]
        ms = jnp.mean(o * o, axis=-1, keepdims=True)
        on = o * lax.rsqrt(ms + EPS) * nw_ref[...]
        z = z_ref[:, sl].astype(F32)
        y_ref[:, sl] = (on * (z * jax.nn.sigmoid(z))).astype(y_ref.dtype)


def _gdn_out(o_f, o_b, proj, nw, base, tb=512):
    Tg = o_f.shape[0]
    return pl.pallas_call(
        _gdn_out_kernel,
        grid=(Tg // tb,),
        in_specs=[
            pl.BlockSpec((tb, 1024), lambda i: (i, 0)),
            pl.BlockSpec((tb, 1024), lambda i: (i, 0)),
            pl.BlockSpec((tb, 1024), lambda i: (base // tb + i, COL["cz"] // 1024)),
            pl.BlockSpec((1, LANES), lambda i: (0, 0)),
        ],
        out_specs=pl.BlockSpec((tb, 1024), lambda i: (i, 0)),
        out_shape=jax.ShapeDtypeStruct((Tg, 1024), BF16),
        compiler_params=_cp(("parallel",), 32),
        name="gdn_out",
    )(o_f, o_b, proj, nw)


def _outproj_kernel(ya_ref, yb_ref, yc_ref, x_ref, w_ref, nw_ref, *rest, with_router):
    dot = functools.partial(jnp.dot, preferred_element_type=F32)
    y = dot(ya_ref[...], w_ref[0:512, :]) + dot(yb_ref[...], w_ref[512:1024, :]) + dot(yc_ref[...], w_ref[1024:, :])
    x = x_ref[...] + y
    ms = jnp.mean(x * x, axis=-1, keepdims=True)
    h = x * lax.rsqrt(ms + EPS) * nw_ref[...]
    if with_router:
        wr_ref, xo_ref, h_ref, lg_ref = rest
        ha, hb, hc = _split3(h)
        wa, wb, wc = wr_ref[0], wr_ref[1], wr_ref[2]
        lg_ref[...] = (dot(ha, wa) + dot(ha, wb) + dot(hb, wa)) + (dot(ha, wc) + dot(hb, wb) + dot(hc, wa))
    else:
        xo_ref, h_ref = rest
    xo_ref[...] = x
    h_ref[...] = h.astype(h_ref.dtype)


def _outproj(ya, yb, yc, x, w, nw, wr3=None, tm=256):
    T = x.shape[0]
    with_router = wr3 is not None
    row = lambda width: pl.BlockSpec((tm, width), lambda i: (i, 0))
    in_specs = [row(512), row(512), row(1024), row(D_MODEL),
                pl.BlockSpec((D_MODEL, D_MODEL), lambda i: (0, 0)),
                pl.BlockSpec((1, D_MODEL), lambda i: (0, 0))]
    out_specs = [row(D_MODEL), row(D_MODEL)]
    out_shape = [jax.ShapeDtypeStruct((T, D_MODEL), F32), jax.ShapeDtypeStruct((T, D_MODEL), F32 if with_router else BF16)]
    args = [ya, yb, yc, x, w, nw]
    if with_router:
        in_specs.append(pl.BlockSpec((3, D_MODEL, LANES), lambda i: (0, 0, 0)))
        out_specs.append(row(LANES))
        out_shape.append(jax.ShapeDtypeStruct((T, LANES), F32))
        args.append(wr3)
    return pl.pallas_call(
        functools.partial(_outproj_kernel, with_router=with_router),
        grid=(T // tm,),
        in_specs=in_specs,
        out_specs=out_specs,
        out_shape=out_shape,
        compiler_params=_cp(("parallel",), 48),
        name="outproj_router" if with_router else "outproj",
    )(*args)


def _ffn_kernel(h_ref, x_ref, wg_ref, wu_ref, wd_ref, o_ref):
    f = pl.program_id(1)
    dot = functools.partial(jnp.dot, preferred_element_type=F32)
    h = h_ref[...]
    g = dot(h, wg_ref[...])
    u = dot(h, wu_ref[...])
    a = (g * jax.nn.sigmoid(g) * u).astype(BF16)
    dn = dot(a, wd_ref[...])

    @pl.when(f == 0)
    def _():
        o_ref[...] = x_ref[...] + dn

    @pl.when(f > 0)
    def _():
        o_ref[...] += dn


def _dense_ffn(h, x, wg, wu, wd, tm=512, tf=512):
    T = x.shape[0]
    F = wg.shape[1]
    return pl.pallas_call(
        _ffn_kernel,
        grid=(T // tm, F // tf),
        in_specs=[
            pl.BlockSpec((tm, D_MODEL), lambda i, f: (i, 0)),
            pl.BlockSpec((tm, D_MODEL), lambda i, f: (i, 0)),
            pl.BlockSpec((D_MODEL, tf), lambda i, f: (0, f)),
            pl.BlockSpec((D_MODEL, tf), lambda i, f: (0, f)),
            pl.BlockSpec((tf, D_MODEL), lambda i, f: (f, 0)),
        ],
        out_specs=pl.BlockSpec((tm, D_MODEL), lambda i, f: (i, 0)),
        out_shape=jax.ShapeDtypeStruct((T, D_MODEL), F32),
        compiler_params=_cp(("parallel", "arbitrary"), 48),
        name="dense_ffn",
    )(h, x, wg, wu, wd)


MOE_TM = 512
R_E1, R_E2, R_RANK1, R_RANK2, R_G1, R_G2 = 0, 1, 2, 3, 4, 5


def _route_kernel(lg_ref, r_ref, cnt_ref, carry_ref, *, tm):
    @pl.when(pl.program_id(0) == 0)
    def _():
        carry_ref[...] = jnp.zeros(carry_ref.shape, F32)

    lg = lg_ref[...]
    lane = lax.broadcasted_iota(jnp.int32, lg.shape, 1).astype(F32)
    l1 = jnp.where(lane < N_EXPERTS, lg, -jnp.inf)
    m1 = jnp.max(l1, axis=-1, keepdims=True)
    i1 = jnp.min(jnp.where(l1 == m1, lane, float(LANES)), axis=-1, keepdims=True)
    l2 = jnp.where(lane == i1, -jnp.inf, l1)
    m2 = jnp.max(l2, axis=-1, keepdims=True)
    i2 = jnp.min(jnp.where(l2 == m2, lane, float(LANES)), axis=-1, keepdims=True)
    e21 = jnp.exp(m2 - m1)
    g1 = 1.0 / (1.0 + e21)
    g2 = e21 / (1.0 + e21)
    oh1 = lane == i1
    oh2 = lane == i2
    oh = jnp.where(oh1 | oh2, 1.0, 0.0).astype(F32)
    r = lax.broadcasted_iota(jnp.int32, (tm, tm), 0)
    c = lax.broadcasted_iota(jnp.int32, (tm, tm), 1)
    tri = jnp.where(c < r, 1.0, 0.0).astype(BF16)
    before = jnp.dot(tri, oh.astype(BF16), preferred_element_type=F32) + carry_ref[...]
    rank1 = jnp.sum(jnp.where(oh1, before, 0.0), axis=-1, keepdims=True)
    rank2 = jnp.sum(jnp.where(oh2, before, 0.0), axis=-1, keepdims=True)
    carry_ref[...] += jnp.sum(oh, axis=0, keepdims=True)
    cnt_ref[...] = carry_ref[...]
    rec = jnp.where(lane == R_E1, i1, 0.0)
    rec = jnp.where(lane == R_E2, i2, rec)
    rec = jnp.where(lane == R_RANK1, rank1, rec)
    rec = jnp.where(lane == R_RANK2, rank2, rec)
    rec = jnp.where(lane == R_G1, g1, rec)
    rec = jnp.where(lane == R_G2, g2, rec)
    r_ref[...] = rec


def _route(logits, tm=512):
    T = logits.shape[0]
    return pl.pallas_call(
        functools.partial(_route_kernel, tm=tm),
        grid=(T // tm,),
        in_specs=[pl.BlockSpec((tm, LANES), lambda i: (i, 0))],
        out_specs=[pl.BlockSpec((tm, LANES), lambda i: (i, 0)), pl.BlockSpec((1, LANES), lambda i: (0, 0))],
        out_shape=[jax.ShapeDtypeStruct((T, LANES), F32), jax.ShapeDtypeStruct((1, LANES), F32)],
        scratch_shapes=[pltpu.VMEM((1, LANES), F32)],
        compiler_params=_cp(("arbitrary",), 32),
        name="moe_route",
    )(logits)


def _positions_kernel(r_ref, off_ref, p_ref):
    rec = r_ref[...]
    off = off_ref[...]
    lane = lax.broadcasted_iota(jnp.int32, rec.shape, 1)
    e1 = rec[:, R_E1:R_E1 + 1].astype(jnp.int32)
    e2 = rec[:, R_E2:R_E2 + 1].astype(jnp.int32)
    p1 = jnp.sum(jnp.where(lane == e1, off, 0.0), axis=-1, keepdims=True) + rec[:, R_RANK1:R_RANK1 + 1]
    p2 = jnp.sum(jnp.where(lane == e2, off, 0.0), axis=-1, keepdims=True) + rec[:, R_RANK2:R_RANK2 + 1]
    out = jnp.where(lane == 0, p1, jnp.where(lane == 1, p2, 0.0))
    p_ref[...] = out.astype(jnp.int32)


def _positions(rec, off_row, tm=512):
    T = rec.shape[0]
    return pl.pallas_call(
        _positions_kernel,
        grid=(T // tm,),
        in_specs=[pl.BlockSpec((tm, LANES), lambda i: (i, 0)), pl.BlockSpec((1, LANES), lambda i: (0, 0))],
        out_specs=pl.BlockSpec((tm, LANES), lambda i: (i, 0)),
        out_shape=jax.ShapeDtypeStruct((T, LANES), jnp.int32),
        compiler_params=_cp(("parallel",), 32),
        name="moe_positions",
    )(rec, off_row)


def _invert_kernel(pos_ref, src_ref, *, tm, n_rows):
    i = pl.program_id(0)

    @pl.when(i == 0)
    def _():
        def clear(r, c):
            src_ref[r] = 0
            return c

        lax.fori_loop(0, n_rows, clear, 0)

    def body(r, c):
        t = i * tm + r
        src_ref[pos_ref[2 * r]] = t
        src_ref[pos_ref[2 * r + 1]] = t
        return c

    lax.fori_loop(0, tm, body, 0)


def _invert(pos_flat, n_rows, tm=512):
    T = pos_flat.shape[0] // 2
    return pl.pallas_call(
        functools.partial(_invert_kernel, tm=tm, n_rows=n_rows),
        grid=(T // tm,),
        in_specs=[pl.BlockSpec((2 * tm,), lambda i: (i,), memory_space=pltpu.SMEM)],
        out_specs=pl.BlockSpec(memory_space=pltpu.SMEM),
        out_shape=jax.ShapeDtypeStruct((n_rows,), jnp.int32),
        compiler_params=_cp(("arbitrary",), 32),
        name="moe_invert",
    )(pos_flat)


def _gather_kernel(src_ref, h_ref, o_ref, sem, *, tm):
    def row_copy(r):
        return pltpu.make_async_copy(h_ref.at[pl.ds(src_ref[r], 1), :], o_ref.at[pl.ds(r, 1), :], sem)

    def start(r, c):
        row_copy(r).start()
        return c

    def wait(r, c):
        row_copy(r).wait()
        return c

    lax.fori_loop(0, tm, start, 0)
    lax.fori_loop(0, tm, wait, 0)


def _gather_rows(src, h, tm=512):
    n_rows = src.shape[0]
    return pl.pallas_call(
        functools.partial(_gather_kernel, tm=tm),
        grid=(n_rows // tm,),
        in_specs=[
            pl.BlockSpec((tm,), lambda i: (i,), memory_space=pltpu.SMEM),
            pl.BlockSpec(memory_space=pl.ANY),
        ],
        out_specs=pl.BlockSpec((tm, D_MODEL), lambda i: (i, 0)),
        out_shape=jax.ShapeDtypeStruct((n_rows, D_MODEL), h.dtype),
        scratch_shapes=[pltpu.SemaphoreType.DMA(())],
        compiler_params=_cp(("arbitrary",), 32),
        name="moe_gather",
    )(src, h)


def _expert_kernel(te_ref, na_ref, x_ref, wg_ref, wu_ref, wd_ref, o_ref, hb_ref):
    i = pl.program_id(0)
    f = pl.program_id(1)
    dot = functools.partial(jnp.dot, preferred_element_type=F32)

    @pl.when(i < na_ref[0])
    def _():
        @pl.when(f == 0)
        def _():
            hb_ref[...] = x_ref[...].astype(BF16)

        h = hb_ref[...]
        g = dot(h, wg_ref[0])
        u = dot(h, wu_ref[0])
        a = (g * jax.nn.sigmoid(g) * u).astype(BF16)
        dn = dot(a, wd_ref[0])

        @pl.when(f == 0)
        def _():
            o_ref[...] = dn

        @pl.when(f > 0)
        def _():
            o_ref[...] += dn

    @pl.when((i >= na_ref[0]) & (f == 0))
    def _():
        o_ref[...] = jnp.zeros(o_ref.shape, F32)


def _expert_ffn(tile_expert, n_active, xs, wg, wu, wd, tm=MOE_TM, tf=512):
    n_tiles = xs.shape[0] // tm
    F = wg.shape[2]
    nf = F // tf

    def row(i, f, te, na):
        return (jnp.minimum(i, na[0] - 1), 0)

    def fcol(i, f, na):
        return jnp.where(i < na[0], f, nf - 1)

    def expert(i, te, na):
        return te[jnp.minimum(i, na[0] - 1)]

    grid_spec = pltpu.PrefetchScalarGridSpec(
        num_scalar_prefetch=2,
        grid=(n_tiles, nf),
        in_specs=[
            pl.BlockSpec((tm, D_MODEL), row),
            pl.BlockSpec((1, D_MODEL, tf), lambda i, f, te, na: (expert(i, te, na), 0, fcol(i, f, na))),
            pl.BlockSpec((1, D_MODEL, tf), lambda i, f, te, na: (expert(i, te, na), 0, fcol(i, f, na))),
            pl.BlockSpec((1, tf, D_MODEL), lambda i, f, te, na: (expert(i, te, na), fcol(i, f, na), 0)),
        ],
        out_specs=pl.BlockSpec((tm, D_MODEL), lambda i, f, te, na: (i, 0)),
        scratch_shapes=[pltpu.VMEM((tm, D_MODEL), BF16)],
    )
    return pl.pallas_call(
        _expert_kernel,
        grid_spec=grid_spec,
        out_shape=jax.ShapeDtypeStruct(xs.shape, F32),
        compiler_params=_cp(("arbitrary", "arbitrary"), 48),
        name="moe_experts",
    )(tile_expert, n_active, xs, wg, wu, wd)


def _combine_kernel(pos_ref, ys_ref, x_ref, r_ref, nw_ref, o_ref, y1_ref, y2_ref, sem, *, tm, final_norm):
    def row_copy(r, s):
        dst = (y1_ref, y2_ref)[s].at[pl.ds(r, 1), :]
        src = ys_ref.at[pl.ds(pos_ref[2 * r + s], 1), :]
        return pltpu.make_async_copy(src, dst, sem)

    def start(r, c):
        row_copy(r, 0).start()
        row_copy(r, 1).start()
        return c

    def wait(r, c):
        row_copy(r, 0).wait()
        row_copy(r, 1).wait()
        return c

    lax.fori_loop(0, tm, start, 0)
    lax.fori_loop(0, tm, wait, 0)
    rec = r_ref[...]
    g1 = rec[:, R_G1:R_G1 + 1]
    g2 = rec[:, R_G2:R_G2 + 1]
    x = x_ref[...] + (g1 * y1_ref[...] + g2 * y2_ref[...])
    if final_norm:
        ms = jnp.mean(x * x, axis=-1, keepdims=True)
        x = x * lax.rsqrt(ms + EPS) * nw_ref[...]
    o_ref[...] = x


def _combine(pos_flat, ys, x, rec, nw, base, n_rows, final_norm, tm=256):
    rb = base // tm
    return pl.pallas_call(
        functools.partial(_combine_kernel, tm=tm, final_norm=final_norm),
        grid=(n_rows // tm,),
        in_specs=[
            pl.BlockSpec((2 * tm,), lambda i: (rb + i,), memory_space=pltpu.SMEM),
            pl.BlockSpec(memory_space=pl.ANY),
            pl.BlockSpec((tm, D_MODEL), lambda i: (rb + i, 0)),
            pl.BlockSpec((tm, LANES), lambda i: (rb + i, 0)),
            pl.BlockSpec((1, D_MODEL), lambda i: (0, 0)),
        ],
        out_specs=pl.BlockSpec((tm, D_MODEL), lambda i: (i, 0)),
        out_shape=jax.ShapeDtypeStruct((n_rows, D_MODEL), F32),
        scratch_shapes=[
            pltpu.VMEM((tm, D_MODEL), F32),
            pltpu.VMEM((tm, D_MODEL), F32),
            pltpu.SemaphoreType.DMA(()),
        ],
        compiler_params=_cp(("arbitrary",), 32),
        name="moe_combine",
    )(pos_flat, ys, x, rec, nw)


def _permute_w_in(w):
    starts = np.cumsum((0,) + REF_SPLITS)
    seg = {n: w[:, starts[i]:starts[i + 1]] for i, n in enumerate(REF_NAMES)}
    main = jnp.concatenate([seg[n] for n in MY_ORDER], axis=1).astype(BF16)
    cab = jnp.pad(seg["cab"], ((0, 0), (0, LANES - 32))).astype(BF16)
    return main, cab


def _lane_row(v):
    v = v.reshape(1, -1).astype(F32)
    return jnp.pad(v, ((0, 0), (0, LANES - v.shape[1])))


def _mixers(proj, cab, groups, p):
    ya, yb, yc = [], [], []
    bias = _na_bias_tables(p["rpb"])
    for base, B, S in groups:
        cos, sin = _rope_tables(S)
        qa, ka = _attn_prep(proj, cos, sin, p["q_norm"], p["k_norm"], base, B, S)
        ya.append(_flash(qa, ka, proj, base, B, S))
        yb.append(_na(proj, bias, base, B, S))
        qn, kn, vc, gates = _gdn_prep(proj, cab, p["conv_w"], p["nega"], p["dtb"], base, B, S)
        gates_t = gates.reshape(B * S // CHUNK, CHUNK, LANES).transpose(0, 2, 1)
        o_f, o_b = _gdn_scan(qn, kn, vc, gates, gates_t, B, S)
        yc.append(_gdn_out(o_f, o_b, proj, p["out_norm"], base))
    return jnp.concatenate(ya), jnp.concatenate(yb), jnp.concatenate(yc)


def _moe(h, x, logits, wg, wu, wd, final_nw, groups):
    T = h.shape[0]
    rec, cnt = _route(logits)
    counts = cnt[0, :N_EXPERTS].astype(jnp.int32)
    padded = ((counts + MOE_TM - 1) // MOE_TM) * MOE_TM
    ends = jnp.cumsum(padded)
    off = ends - padded
    n_rows = 2 * T + N_EXPERTS * MOE_TM
    n_tiles = n_rows // MOE_TM
    n_active = (ends[-1] // MOE_TM).astype(jnp.int32).reshape(1)
    tile_start = jnp.arange(n_tiles, dtype=jnp.int32) * MOE_TM
    tile_expert = jnp.minimum(jnp.sum(tile_start[:, None] >= ends[None, :], axis=1), N_EXPERTS - 1).astype(jnp.int32)
    off_row = _lane_row(off.astype(F32))
    pos = _positions(rec, off_row)[:, :2].reshape(-1)
    xs = _gather_rows(_invert(pos, n_rows), h)
    ys = _expert_ffn(tile_expert, n_active, xs, wg, wu, wd)
    outs = []
    for base, B, S in groups:
        y = _combine(pos, ys, x, rec, final_nw, base, B * S, final_norm=True)
        outs.append(y.reshape(B, S, D_MODEL))
    return outs


def kernel(x_prompt, x_sample, mix_norm, w_in, q_norm, k_norm, rpb, conv_w, a_log, dt_bias, out_norm, w_out,
           ffn_norm, dense_gate, dense_up, dense_down, router, moe_gate, moe_up, moe_down, final_norm):
    depth = mix_norm.shape[0]
    assert depth == 2, "layer 0 dense FFN, layer 1 MoE FFN followed by the final norm"
    Bp, Sp, _ = x_prompt.shape
    Bs, Ss, _ = x_sample.shape
    groups = ((0, Bp, Sp), (Bp * Sp, Bs, Ss))
    x = jnp.concatenate([x_prompt.reshape(Bp * Sp, D_MODEL), x_sample.reshape(Bs * Ss, D_MODEL)], axis=0)
    outs = None
    for l in range(depth):
        w_main, w_cab = _permute_w_in(w_in[l])
        p = {
            "q_norm": _lane_row(q_norm[l]), "k_norm": _lane_row(k_norm[l]), "rpb": rpb[l],
            "conv_w": conv_w[l].astype(F32),
            "nega": _lane_row(-jnp.exp(a_log[l].astype(F32))), "dtb": _lane_row(dt_bias[l]),
            "out_norm": _lane_row(out_norm[l]),
        }
        proj, cab = _inproj(x, mix_norm[l].reshape(1, D_MODEL), w_main, w_cab)
        ya, yb, yc = _mixers(proj, cab, groups, p)
        wo = w_out[l].astype(BF16)
        fnw = ffn_norm[l].reshape(1, D_MODEL)
        if l % 2 == 0:
            x, h = _outproj(ya, yb, yc, x, wo, fnw)
            x = _dense_ffn(h, x, dense_gate[l // 2].astype(BF16), dense_up[l // 2].astype(BF16),
                           dense_down[l // 2].astype(BF16))
        else:
            wr = jnp.pad(router[l // 2].astype(F32), ((0, 0), (0, LANES - N_EXPERTS)))
            x, h, logits = _outproj(ya, yb, yc, x, wo, fnw, jnp.stack(_split3(wr)))
            outs = _moe(h, x, logits, moe_gate[l // 2].astype(BF16), moe_up[l // 2].astype(BF16),
                        moe_down[l // 2].astype(BF16), final_norm.reshape(1, D_MODEL), groups)
    return tuple(outs)
```

```python
import functools

import numpy as np
import jax
import jax.numpy as jnp
from jax import lax
from jax.experimental import pallas as pl
from jax.experimental.pallas import tpu as pltpu

F32 = jnp.float32
BF16 = jnp.bfloat16
EPS = 1e-6

D_MODEL = 2048
GRID_W = 64
HEAD_DIM = 128
ATT_HEADS = 4
ATT_KV_HEADS = 2
ROPE_THETA = 10000.0
NA_HEADS = 4
NA_WIN_ROWS = 8
NA_WIN_COLS = 16
DN_HEADS = 8
CONV_K = 5
CHUNK = 64
N_EXPERTS = 8
LANES = 128

PROJ_MAIN = 6656
REF_SPLITS = (512, 256, 256, 512, 512, 512, 1024, 1024, 1024, 1024, 32)
REF_NAMES = ("aq", "ak", "av", "bq", "bk", "bv", "cq", "ck", "cv", "cz", "cab")
MY_ORDER = ("cq", "ck", "cv", "cz", "aq", "bq", "bk", "bv", "ak", "av")
COL = {}
_off = 0
for _n in MY_ORDER:
    COL[_n] = _off
    _off += REF_SPLITS[REF_NAMES.index(_n)]
assert _off == PROJ_MAIN

MIB = 1024 * 1024


def _cp(sem, vmem_mib):
    return pltpu.CompilerParams(dimension_semantics=sem, vmem_limit_bytes=vmem_mib * MIB)


def _inproj_kernel(x_ref, nw_ref, w_ref, wc_ref, o_ref, oc_ref, h_ref):
    @pl.when(pl.program_id(1) == 0)
    def _():
        x = x_ref[...]
        ms = jnp.mean(x * x, axis=-1, keepdims=True)
        hb = (x * lax.rsqrt(ms + EPS) * nw_ref[...]).astype(BF16)
        h_ref[...] = hb
        oc_ref[...] = jnp.dot(hb, wc_ref[...], preferred_element_type=F32)

    o_ref[...] = jnp.dot(h_ref[...], w_ref[...], preferred_element_type=F32).astype(o_ref.dtype)


def _inproj(x, nw, w, wc, tm=1024, tn=512):
    T = x.shape[0]
    return pl.pallas_call(
        _inproj_kernel,
        grid=(T // tm, PROJ_MAIN // tn),
        in_specs=[
            pl.BlockSpec((tm, D_MODEL), lambda i, j: (i, 0)),
            pl.BlockSpec((1, D_MODEL), lambda i, j: (0, 0)),
            pl.BlockSpec((D_MODEL, tn), lambda i, j: (0, j)),
            pl.BlockSpec((D_MODEL, LANES), lambda i, j: (0, 0)),
        ],
        out_specs=[
            pl.BlockSpec((tm, tn), lambda i, j: (i, j)),
            pl.BlockSpec((tm, LANES), lambda i, j: (i, 0)),
        ],
        out_shape=[jax.ShapeDtypeStruct((T, PROJ_MAIN), BF16), jax.ShapeDtypeStruct((T, LANES), F32)],
        scratch_shapes=[pltpu.VMEM((tm, D_MODEL), BF16)],
        compiler_params=_cp(("parallel", "arbitrary"), 40),
        name="inproj",
    )(x, nw, w, wc)


def _rope_tables(S):
    half = HEAD_DIM // 2
    quarter = half // 2
    t = jnp.arange(S)
    inv = ROPE_THETA ** (-jnp.arange(quarter, dtype=F32) / quarter)
    ang_r = (t // GRID_W).astype(F32)[:, None] * inv
    ang_c = (t % GRID_W).astype(F32)[:, None] * inv
    cos = jnp.concatenate([jnp.cos(ang_r), jnp.cos(ang_r), jnp.cos(ang_c), jnp.cos(ang_c)], axis=-1)
    sin = jnp.concatenate([-jnp.sin(ang_r), jnp.sin(ang_r), -jnp.sin(ang_c), jnp.sin(ang_c)], axis=-1)
    return cos, sin


def _norm_rope(xh, nw, cos, sin, first_quarter):
    ms = jnp.mean(xh * xh, axis=-1, keepdims=True)
    xn = xh * lax.rsqrt(ms + EPS) * nw
    partner = jnp.where(first_quarter, pltpu.roll(xn, 96, 1), pltpu.roll(xn, 32, 1))
    return xn * cos + partner * sin


def _attn_prep_kernel(q_ref, k_ref, cos_ref, sin_ref, qn_ref, kn_ref, qo_ref, ko_ref):
    cos = cos_ref[...]
    sin = sin_ref[...]
    lane = lax.broadcasted_iota(jnp.int32, cos.shape, 1)
    first_quarter = (lane % 64) < 32
    scale = HEAD_DIM ** -0.5
    for h in range(ATT_HEADS):
        sl = slice(h * HEAD_DIM, (h + 1) * HEAD_DIM)
        y = _norm_rope(q_ref[:, sl].astype(F32), qn_ref[...], cos, sin, first_quarter)
        qo_ref[:, sl] = (y * scale).astype(BF16)
    for h in range(ATT_KV_HEADS):
        sl = slice(h * HEAD_DIM, (h + 1) * HEAD_DIM)
        y = _norm_rope(k_ref[:, sl].astype(F32), kn_ref[...], cos, sin, first_quarter)
        ko_ref[:, sl] = y.astype(BF16)


def _attn_prep(proj, cos, sin, qn, kn, base, B, S, tm=256):
    nb = S // tm
    rb = base // tm
    return pl.pallas_call(
        _attn_prep_kernel,
        grid=(B * nb,),
        in_specs=[
            pl.BlockSpec((tm, 512), lambda i: (rb + i, COL["aq"] // 512)),
            pl.BlockSpec((tm, 256), lambda i: (rb + i, COL["ak"] // 256)),
            pl.BlockSpec((tm, LANES), lambda i: (i % nb, 0)),
            pl.BlockSpec((tm, LANES), lambda i: (i % nb, 0)),
            pl.BlockSpec((1, LANES), lambda i: (0, 0)),
            pl.BlockSpec((1, LANES), lambda i: (0, 0)),
        ],
        out_specs=[
            pl.BlockSpec((tm, 512), lambda i: (i, 0)),
            pl.BlockSpec((tm, 256), lambda i: (i, 0)),
        ],
        out_shape=[jax.ShapeDtypeStruct((B * S, 512), BF16), jax.ShapeDtypeStruct((B * S, 256), BF16)],
        compiler_params=_cp(("parallel",), 32),
        name="attn_prep",
    )(proj, proj, cos, sin, qn, kn)


def _flash_kernel(q_ref, k_ref, v_ref, o_ref, m_ref, l_ref, acc_ref, *, tq, tk, S):
    q = q_ref[...]
    q2 = jnp.concatenate([q[:, :HEAD_DIM], q[:, HEAD_DIM:]], axis=0)
    m_ref[...] = jnp.full(m_ref.shape, -jnp.inf, F32)
    l_ref[...] = jnp.zeros(l_ref.shape, F32)
    acc_ref[...] = jnp.zeros(acc_ref.shape, F32)

    def body(j, carry):
        start = pl.multiple_of(j * tk, tk)
        k = k_ref[pl.ds(start, tk), :]
        v = v_ref[pl.ds(start, tk), :]
        s = lax.dot_general(q2, k, (((1,), (1,)), ((), ())), preferred_element_type=F32)
        blocks = [s[:, b * HEAD_DIM:(b + 1) * HEAD_DIM] for b in range(tk // HEAD_DIM)]
        m_prev = m_ref[...]
        m_new = jnp.maximum(m_prev, jnp.max(functools.reduce(jnp.maximum, blocks), axis=-1, keepdims=True))
        alpha = jnp.exp(m_prev - m_new)
        pb = [jnp.exp(b - m_new) for b in blocks]
        l_ref[...] = alpha * l_ref[...] + jnp.sum(functools.reduce(jnp.add, pb), axis=-1, keepdims=True)
        p = jnp.concatenate([b.astype(BF16) for b in pb], axis=1)
        acc_ref[...] = alpha * acc_ref[...] + jnp.dot(p, v, preferred_element_type=F32)
        m_ref[...] = m_new
        return carry

    lax.fori_loop(0, S // tk, body, 0)
    o = acc_ref[...] / l_ref[...]
    o_ref[...] = jnp.concatenate([o[:tq], o[tq:]], axis=1).astype(o_ref.dtype)


def _flash(qa, ka, proj, base, B, S, tq=512, tk=1024):
    nq = S // tq
    vb = COL["av"] // HEAD_DIM
    return pl.pallas_call(
        functools.partial(_flash_kernel, tq=tq, tk=tk, S=S),
        grid=(B, ATT_KV_HEADS, nq),
        in_specs=[
            pl.BlockSpec((tq, 256), lambda b, h, i: (b * nq + i, h)),
            pl.BlockSpec((S, HEAD_DIM), lambda b, h, i: (b, h)),
            pl.BlockSpec((S, HEAD_DIM), lambda b, h, i: (base // S + b, vb + h)),
        ],
        out_specs=pl.BlockSpec((tq, 256), lambda b, h, i: (b * nq + i, h)),
        out_shape=jax.ShapeDtypeStruct((B * S, 512), BF16),
        scratch_shapes=[
            pltpu.VMEM((2 * tq, HEAD_DIM), F32),
            pltpu.VMEM((2 * tq, HEAD_DIM), F32),
            pltpu.VMEM((2 * tq, HEAD_DIM), F32),
        ],
        compiler_params=_cp(("parallel", "parallel", "arbitrary"), 40),
        name="flash_gqa",
    )(qa, ka, proj)


NA_QROWS = 8
NA_KROWS = 16
NA_TQ = NA_QROWS * GRID_W
NA_TK = NA_KROWS * GRID_W
NA_NEG = -1e30


def _na_bias_tables(rpb):
    qr = np.arange(NA_QROWS)[:, None]
    kr = np.arange(NA_KROWS)[None, :]
    c = np.arange(GRID_W)[:, None]
    kc = np.arange(GRID_W)[None, :]
    cs = np.clip(c - NA_WIN_COLS // 2, 0, GRID_W - NA_WIN_COLS)
    col_ok = (kc >= cs) & (kc < cs + NA_WIN_COLS)
    dc = np.clip(kc - c, -(NA_WIN_COLS - 1), NA_WIN_COLS - 1) + NA_WIN_COLS - 1
    col_sel = (dc[..., None] == np.arange(2 * NA_WIN_COLS - 1)).astype(np.float32)
    by_col = jnp.einsum("hab,cmb->hacm", rpb.astype(F32), col_sel, precision=lax.Precision.HIGHEST)
    tables = []
    for off, lo in ((0, np.maximum(qr - 4, 0)), (-4, qr + 0), (-8, 8 + np.minimum(qr - 4, 0))):
        row_ok = (kr >= lo) & (kr < lo + NA_WIN_ROWS)
        dr = np.clip(off + kr - qr + NA_WIN_ROWS - 1, 0, 2 * NA_WIN_ROWS - 2)
        row_sel = (dr[..., None] == np.arange(2 * NA_WIN_ROWS - 1)).astype(np.float32)
        bias = jnp.einsum("qka,hacm->hqckm", row_sel, by_col, precision=lax.Precision.HIGHEST)
        ok = row_ok[:, None, :, None] & col_ok[None, :, None, :]
        tables.append(jnp.where(jnp.asarray(ok)[None], bias, NA_NEG).reshape(NA_HEADS, NA_TQ, NA_TK))
    return jnp.stack(tables)


def _na_kernel(q_ref, k_ref, v_ref, b_ref, o_ref, *, S):
    j = pl.program_id(2)
    start = pl.multiple_of(jnp.clip(j * NA_TQ - 4 * GRID_W, 0, S - NA_TK), 4 * GRID_W)
    k = k_ref[pl.ds(start, NA_TK), :]
    v = v_ref[pl.ds(start, NA_TK), :]
    s = lax.dot_general(q_ref[...], k, (((1,), (1,)), ((), ())), preferred_element_type=F32)
    s = s * (HEAD_DIM ** -0.5) + b_ref[0, 0]
    blocks = [s[:, b * LANES:(b + 1) * LANES] for b in range(NA_TK // LANES)]
    m = jnp.broadcast_to(jnp.max(functools.reduce(jnp.maximum, blocks), axis=-1, keepdims=True), (NA_TQ, LANES))
    pb = [jnp.exp(b - m) for b in blocks]
    l = jnp.sum(functools.reduce(jnp.add, pb), axis=-1, keepdims=True)
    p = jnp.concatenate([b.astype(BF16) for b in pb], axis=1)
    o = jnp.dot(p, v, preferred_element_type=F32)
    o_ref[...] = (o / l).astype(o_ref.dtype)


def _na(proj, bias, base, B, S):
    assert S % NA_TQ == 0 and S >= NA_TK
    nj = S // NA_TQ
    qb, kb, vb = COL["bq"] // HEAD_DIM, COL["bk"] // HEAD_DIM, COL["bv"] // HEAD_DIM

    def case(j):
        return jnp.where(j == 0, 0, jnp.where(j == nj - 1, 2, 1))

    return pl.pallas_call(
        functools.partial(_na_kernel, S=S),
        grid=(B, NA_HEADS, nj),
        in_specs=[
            pl.BlockSpec((NA_TQ, HEAD_DIM), lambda b, h, j: (base // NA_TQ + b * nj + j, qb + h)),
            pl.BlockSpec((S, HEAD_DIM), lambda b, h, j: (base // S + b, kb + h)),
            pl.BlockSpec((S, HEAD_DIM), lambda b, h, j: (base // S + b, vb + h)),
            pl.BlockSpec((1, 1, NA_TQ, NA_TK), lambda b, h, j: (case(j), h, 0, 0)),
        ],
        out_specs=pl.BlockSpec((NA_TQ, HEAD_DIM), lambda b, h, j: (b * nj + j, h)),
        out_shape=jax.ShapeDtypeStruct((B * S, NA_HEADS * HEAD_DIM), BF16),
        compiler_params=_cp(("parallel", "parallel", "arbitrary"), 40),
        name="nbr_attn",
    )(proj, proj, proj, bias)


G_GC, G_BETA, G_EGC, G_EKD, G_ELAST, G_BEXP = 0, 16, 32, 48, 64, 80
GDN_TB = 256
HALO = 16


def _split3(x):
    a = x.astype(BF16)
    r = x - a.astype(F32)
    b = r.astype(BF16)
    c = (r - b.astype(F32)).astype(BF16)
    return a, b, c


def _dot_exact_lhs(m_bf16, x):
    a, b, c = _split3(x)
    d = functools.partial(jnp.dot, preferred_element_type=F32)
    return d(m_bf16, a) + d(m_bf16, b) + d(m_bf16, c)


def _gdn_prep_kernel(x_ref, xp_ref, xn_ref, cw_ref, cab_ref, nega_ref, dtb_ref,
                     q_ref, k_ref, v_ref, g_ref, ext_ref, *, nb):
    i = pl.program_id(0)
    tb = GDN_TB
    first = (i % nb) == 0
    last = (i % nb) == nb - 1
    ext_ref[0:HALO, :] = jnp.where(first, 0.0, xp_ref[...].astype(F32))
    ext_ref[HALO:HALO + tb, :] = x_ref[...].astype(F32)
    ext_ref[HALO + tb:, :] = jnp.where(last, 0.0, xn_ref[...].astype(F32))
    outs = (q_ref, k_ref, v_ref)
    for c in range(3 * DN_HEADS):
        sl = slice(c * LANES, (c + 1) * LANES)
        acc = jnp.zeros((tb, LANES), F32)
        for t in range(CONV_K):
            acc = acc + cw_ref[t:t + 1, sl] * ext_ref[HALO - CONV_K // 2 + t:HALO - CONV_K // 2 + t + tb, sl]
        y = acc * jax.nn.sigmoid(acc)
        which, h = divmod(c, DN_HEADS)
        if which < 2:
            y = y * lax.rsqrt(jnp.sum(y * y, axis=-1, keepdims=True) + EPS)
        if which == 0:
            y = y * (HEAD_DIM ** -0.5)
        outs[which][:, h * LANES:(h + 1) * LANES] = y.astype(BF16)

    cab = cab_ref[...]
    lane = lax.broadcasted_iota(jnp.int32, cab.shape, 1)
    z = cab + dtb_ref[...]
    softplus = jnp.maximum(z, 0.0) + jnp.log1p(jnp.exp(-jnp.abs(z)))
    g = jnp.where(lane < 16, nega_ref[...] * softplus, 0.0)
    beta = jnp.where((lane >= 16) & (lane < 32), jax.nn.sigmoid(cab), 0.0)
    r = lax.broadcasted_iota(jnp.int32, (tb, tb), 0)
    cidx = lax.broadcasted_iota(jnp.int32, (tb, tb), 1)
    same = (r // CHUNK) == (cidx // CHUNK)
    allm32 = jnp.where(same, 1.0, 0.0).astype(F32)
    low = jnp.where(cidx <= r, allm32, 0.0).astype(BF16)
    upp = jnp.where(cidx >= r, allm32, 0.0).astype(BF16)
    allm = allm32.astype(BF16)
    gc = jnp.where(lane < 8, _dot_exact_lhs(low, g), _dot_exact_lhs(upp, g))
    tot = _dot_exact_lhs(allm, g)
    egc = jnp.exp(gc)
    out = gc
    out = out + beta
    out = out + pltpu.roll(jnp.where(lane < 16, egc, 0.0), G_EGC, 1)
    out = out + pltpu.roll(jnp.where(lane < 16, jnp.exp(tot - gc), 0.0), G_EKD, 1)
    out = out + pltpu.roll(jnp.where(lane < 16, jnp.exp(tot), 0.0), G_ELAST, 1)
    bexp = beta * pltpu.roll(jnp.where(lane < 16, egc, 0.0), 16, 1)
    out = out + pltpu.roll(bexp, G_BEXP - 16, 1)
    g_ref[...] = out


def _gdn_prep(proj, cab, conv_w, nega, dtb, base, B, S):
    tb = GDN_TB
    nb = S // tb
    rb = base // tb
    hb = tb // HALO
    nrows = proj.shape[0] // HALO
    width = 3 * DN_HEADS * LANES
    return pl.pallas_call(
        functools.partial(_gdn_prep_kernel, nb=nb),
        grid=(B * nb,),
        in_specs=[
            pl.BlockSpec((tb, width), lambda i: (rb + i, 0)),
            pl.BlockSpec((HALO, width), lambda i: (jnp.maximum((rb + i) * hb - 1, 0), 0)),
            pl.BlockSpec((HALO, width), lambda i: (jnp.minimum((rb + i + 1) * hb, nrows - 1), 0)),
            pl.BlockSpec((CONV_K, width), lambda i: (0, 0)),
            pl.BlockSpec((tb, LANES), lambda i: (rb + i, 0)),
            pl.BlockSpec((1, LANES), lambda i: (0, 0)),
            pl.BlockSpec((1, LANES), lambda i: (0, 0)),
        ],
        out_specs=[
            pl.BlockSpec((tb, 1024), lambda i: (i, 0)),
            pl.BlockSpec((tb, 1024), lambda i: (i, 0)),
            pl.BlockSpec((tb, 1024), lambda i: (i, 0)),
            pl.BlockSpec((tb, LANES), lambda i: (i, 0)),
        ],
        out_shape=[jax.ShapeDtypeStruct((B * S, 1024), BF16)] * 3 + [jax.ShapeDtypeStruct((B * S, LANES), F32)],
        scratch_shapes=[pltpu.VMEM((tb + 2 * HALO, width), F32)],
        compiler_params=_cp(("parallel",), 40),
        name="gdn_prep",
    )(proj, proj, proj, conv_w, cab, nega, dtb)


def _gdn_scan_kernel(qf_ref, kf_ref, vf_ref, gf_ref, gtf_ref, qb_ref, kb_ref, vb_ref, gb_ref, gtb_ref,
                     of_ref, ob_ref, s_ref):
    @pl.when(pl.program_id(1) == 0)
    def _():
        s_ref[...] = jnp.zeros(s_ref.shape, F32)

    C = CHUNK
    refs = ((qf_ref, kf_ref, vf_ref, gf_ref, gtf_ref, of_ref), (qb_ref, kb_ref, vb_ref, gb_ref, gtb_ref, ob_ref))
    chains = [(d, h) for h in range(DN_HEADS) for d in (0, 1)]
    ri = lax.broadcasted_iota(jnp.int32, (C, C), 0)
    ci = lax.broadcasted_iota(jnp.int32, (C, C), 1)
    incl = (ri >= ci, ri <= ci)
    strict = (ri > ci, ri < ci)
    eye = jnp.where(ri == ci, 1.0, 0.0).astype(F32)
    dot = functools.partial(jnp.dot, preferred_element_type=F32)
    gcol = (gf_ref[...], gb_ref[...])
    grow = (gtf_ref[0], gtb_ref[0])

    def colb(field, ch, width):
        d, h = ch
        idx = field + d * DN_HEADS + h
        return jnp.broadcast_to(gcol[d][:, idx:idx + 1], (C, width))

    def rowb(field, ch):
        d, h = ch
        idx = field + d * DN_HEADS + h
        return jnp.broadcast_to(grow[d][idx:idx + 1, :], (C, C))

    def sl(ch):
        return slice(ch[1] * LANES, (ch[1] + 1) * LANES)

    def sidx(ch):
        return ch[0] * DN_HEADS + ch[1]

    q = {ch: refs[ch[0]][0][:, sl(ch)] for ch in chains}
    k = {ch: refs[ch[0]][1][:, sl(ch)] for ch in chains}
    v = {ch: refs[ch[0]][2][:, sl(ch)] for ch in chains}
    kq = {ch: lax.dot_general(jnp.concatenate([k[ch], q[ch]], axis=0), k[ch], (((1,), (1,)), ((), ())),
                              preferred_element_type=F32) for ch in chains}
    sb = {ch: s_ref[sidx(ch)].astype(BF16) for ch in chains}
    qs = {ch: dot(q[ch], sb[ch]) for ch in chains}
    decay = {ch: jnp.where(incl[ch[0]], jnp.exp(colb(G_GC, ch, C) - rowb(G_GC, ch)), 0.0) for ch in chains}
    n = {ch: jnp.where(strict[ch[0]], -(kq[ch][:C] * colb(G_BETA, ch, C) * decay[ch]), 0.0) for ch in chains}
    qk = {ch: (kq[ch][C:] * decay[ch]).astype(BF16) for ch in chains}
    x = {ch: eye + n[ch] for ch in chains}
    nb = {ch: n[ch].astype(BF16) for ch in chains}
    p = {ch: dot(nb[ch], nb[ch]) for ch in chains}
    steps = int(np.log2(C)) - 1
    for it in range(steps):
        pb = {ch: p[ch].astype(BF16) for ch in chains}
        if it < steps - 1:
            xp = {ch: dot(pb[ch], jnp.concatenate([x[ch].astype(BF16), pb[ch]], axis=1)) for ch in chains}
            x = {ch: x[ch] + xp[ch][:, :C] for ch in chains}
            p = {ch: xp[ch][:, C:] for ch in chains}
        else:
            x = {ch: x[ch] + dot(pb[ch], x[ch].astype(BF16)) for ch in chains}
    u = {ch: dot((x[ch] * rowb(G_BETA, ch)).astype(BF16), v[ch]) for ch in chains}
    w = {ch: dot((x[ch] * rowb(G_BEXP, ch)).astype(BF16), k[ch]).astype(BF16) for ch in chains}
    kd = {ch: (k[ch].astype(F32) * colb(G_EKD, ch, LANES)).T.astype(BF16) for ch in chains}
    v_new = {ch: (u[ch] - dot(w[ch], sb[ch])).astype(BF16) for ch in chains}
    for ch in chains:
        refs[ch[0]][5][:, sl(ch)] = colb(G_EGC, ch, LANES) * qs[ch] + dot(qk[ch], v_new[ch])
    for ch in chains:
        d, h = ch
        idx = G_ELAST + d * DN_HEADS + h
        glast = jnp.broadcast_to(gcol[d][0:1, idx:idx + 1], (LANES, LANES))
        s_ref[sidx(ch)] = s_ref[sidx(ch)] * glast + dot(kd[ch], v_new[ch])


def _gdn_scan(qn, kn, vc, gates, gates_t, B, S):
    n = S // CHUNK
    fwd = lambda b, c: b * n + c
    bwd = lambda b, c: b * n + (n - 1 - c)

    def specs(idx):
        return [
            pl.BlockSpec((CHUNK, 1024), lambda b, c: (idx(b, c), 0)),
            pl.BlockSpec((CHUNK, 1024), lambda b, c: (idx(b, c), 0)),
            pl.BlockSpec((CHUNK, 1024), lambda b, c: (idx(b, c), 0)),
            pl.BlockSpec((CHUNK, LANES), lambda b, c: (idx(b, c), 0)),
            pl.BlockSpec((1, LANES, CHUNK), lambda b, c: (idx(b, c), 0, 0)),
        ]

    return pl.pallas_call(
        _gdn_scan_kernel,
        grid=(B, n),
        in_specs=specs(fwd) + specs(bwd),
        out_specs=[pl.BlockSpec((CHUNK, 1024), lambda b, c: (fwd(b, c), 0)),
                   pl.BlockSpec((CHUNK, 1024), lambda b, c: (bwd(b, c), 0))],
        out_shape=[jax.ShapeDtypeStruct((B * S, 1024), F32)] * 2,
        scratch_shapes=[pltpu.VMEM((2 * DN_HEADS, LANES, LANES), F32)],
        compiler_params=_cp(("parallel", "arbitrary"), 32),
        name="gdn_scan",
    )(qn, kn, vc, gates, gates_t, qn, kn, vc, gates, gates_t)


def _gdn_out_kernel(of_ref, ob_ref, z_ref, nw_ref, y_ref):
    for h in range(DN_HEADS):
        sl = slice(h * LANES, (h + 1) * LANES)
        o = of_ref[:, sl] + ob_ref[:, sl]
        ms = jnp.mean(o * o, axis=-1, keepdims=True)
        on = o * lax.rsqrt(ms + EPS) * nw_ref[...]
        z = z_ref[:, sl].astype(F32)
        y_ref[:, sl] = (on * (z * jax.nn.sigmoid(z))).astype(y_ref.dtype)


def _gdn_out(o_f, o_b, proj, nw, base, tb=512):
    Tg = o_f.shape[0]
    return pl.pallas_call(
        _gdn_out_kernel,
        grid=(Tg // tb,),
        in_specs=[
            pl.BlockSpec((tb, 1024), lambda i: (i, 0)),
            pl.BlockSpec((tb, 1024), lambda i: (i, 0)),
            pl.BlockSpec((tb, 1024), lambda i: (base // tb + i, COL["cz"] // 1024)),
            pl.BlockSpec((1, LANES), lambda i: (0, 0)),
        ],
        out_specs=pl.BlockSpec((tb, 1024), lambda i: (i, 0)),
        out_shape=jax.ShapeDtypeStruct((Tg, 1024), BF16),
        compiler_params=_cp(("parallel",), 32),
        name="gdn_out",
    )(o_f, o_b, proj, nw)


def _outproj_kernel(ya_ref, yb_ref, yc_ref, x_ref, w_ref, nw_ref, *rest, with_router):
    dot = functools.partial(jnp.dot, preferred_element_type=F32)
    y = dot(ya_ref[...], w_ref[0:512, :]) + dot(yb_ref[...], w_ref[512:1024, :]) + dot(yc_ref[...], w_ref[1024:, :])
    x = x_ref[...] + y
    ms = jnp.mean(x * x, axis=-1, keepdims=True)
    h = x * lax.rsqrt(ms + EPS) * nw_ref[...]
    if with_router:
        wr_ref, xo_ref, h_ref, lg_ref = rest
        ha, hb, hc = _split3(h)
        wa, wb, wc = wr_ref[0], wr_ref[1], wr_ref[2]
        lg_ref[...] = (dot(ha, wa) + dot(ha, wb) + dot(hb, wa)) + (dot(ha, wc) + dot(hb, wb) + dot(hc, wa))
    else:
        xo_ref, h_ref = rest
    xo_ref[...] = x
    h_ref[...] = h.astype(h_ref.dtype)


def _outproj(ya, yb, yc, x, w, nw, wr3=None, tm=256):
    T = x.shape[0]
    with_router = wr3 is not None
    row = lambda width: pl.BlockSpec((tm, width), lambda i: (i, 0))
    in_specs = [row(512), row(512), row(1024), row(D_MODEL),
                pl.BlockSpec((D_MODEL, D_MODEL), lambda i: (0, 0)),
                pl.BlockSpec((1, D_MODEL), lambda i: (0, 0))]
    out_specs = [row(D_MODEL), row(D_MODEL)]
    out_shape = [jax.ShapeDtypeStruct((T, D_MODEL), F32), jax.ShapeDtypeStruct((T, D_MODEL), F32 if with_router else BF16)]
    args = [ya, yb, yc, x, w, nw]
    if with_router:
        in_specs.append(pl.BlockSpec((3, D_MODEL, LANES), lambda i: (0, 0, 0)))
        out_specs.append(row(LANES))
        out_shape.append(jax.ShapeDtypeStruct((T, LANES), F32))
        args.append(wr3)
    return pl.pallas_call(
        functools.partial(_outproj_kernel, with_router=with_router),
        grid=(T // tm,),
        in_specs=in_specs,
        out_specs=out_specs,
        out_shape=out_shape,
        compiler_params=_cp(("parallel",), 48),
        name="outproj_router" if with_router else "outproj",
    )(*args)


def _ffn_kernel(h_ref, x_ref, wg_ref, wu_ref, wd_ref, o_ref):
    dot = functools.partial(jnp.dot, preferred_element_type=F32)

    @pl.when(pl.program_id(1) == 0)
    def _():
        o_ref[...] = x_ref[...]

    h = h_ref[...]
    g = dot(h, wg_ref[...])
    u = dot(h, wu_ref[...])
    a = (g * jax.nn.sigmoid(g) * u).astype(BF16)
    o_ref[...] += dot(a, wd_ref[...])


def _dense_ffn(h, x, wg, wu, wd, tm=512, tf=512):
    T = x.shape[0]
    F = wg.shape[1]
    return pl.pallas_call(
        _ffn_kernel,
        grid=(T // tm, F // tf),
        in_specs=[
            pl.BlockSpec((tm, D_MODEL), lambda i, f: (i, 0)),
            pl.BlockSpec((tm, D_MODEL), lambda i, f: (i, 0)),
            pl.BlockSpec((D_MODEL, tf), lambda i, f: (0, f)),
            pl.BlockSpec((D_MODEL, tf), lambda i, f: (0, f)),
            pl.BlockSpec((tf, D_MODEL), lambda i, f: (f, 0)),
        ],
        out_specs=pl.BlockSpec((tm, D_MODEL), lambda i, f: (i, 0)),
        out_shape=jax.ShapeDtypeStruct((T, D_MODEL), F32),
        compiler_params=_cp(("parallel", "arbitrary"), 48),
        name="dense_ffn",
    )(h, x, wg, wu, wd)


MOE_TM = 512
R_E1, R_E2, R_RANK1, R_RANK2, R_G1, R_G2 = 0, 1, 2, 3, 4, 5


def _route_kernel(lg_ref, r_ref, cnt_ref, carry_ref, *, tm):
    @pl.when(pl.program_id(0) == 0)
    def _():
        carry_ref[...] = jnp.zeros(carry_ref.shape, F32)

    lg = lg_ref[...]
    lane = lax.broadcasted_iota(jnp.int32, lg.shape, 1).astype(F32)
    l1 = jnp.where(lane < N_EXPERTS, lg, -jnp.inf)
    m1 = jnp.max(l1, axis=-1, keepdims=True)
    i1 = jnp.min(jnp.where(l1 == m1, lane, float(LANES)), axis=-1, keepdims=True)
    l2 = jnp.where(lane == i1, -jnp.inf, l1)
    m2 = jnp.max(l2, axis=-1, keepdims=True)
    i2 = jnp.min(jnp.where(l2 == m2, lane, float(LANES)), axis=-1, keepdims=True)
    e21 = jnp.exp(m2 - m1)
    g1 = 1.0 / (1.0 + e21)
    g2 = e21 / (1.0 + e21)
    oh1 = lane == i1
    oh2 = lane == i2
    oh = jnp.where(oh1 | oh2, 1.0, 0.0).astype(F32)
    r = lax.broadcasted_iota(jnp.int32, (tm, tm), 0)
    c = lax.broadcasted_iota(jnp.int32, (tm, tm), 1)
    tri = jnp.where(c < r, 1.0, 0.0).astype(BF16)
    before = jnp.dot(tri, oh.astype(BF16), preferred_element_type=F32) + carry_ref[...]
    rank1 = jnp.sum(jnp.where(oh1, before, 0.0), axis=-1, keepdims=True)
    rank2 = jnp.sum(jnp.where(oh2, before, 0.0), axis=-1, keepdims=True)
    carry_ref[...] += jnp.sum(oh, axis=0, keepdims=True)
    cnt_ref[...] = carry_ref[...]
    rec = jnp.where(lane == R_E1, i1, 0.0)
    rec = jnp.where(lane == R_E2, i2, rec)
    rec = jnp.where(lane == R_RANK1, rank1, rec)
    rec = jnp.where(lane == R_RANK2, rank2, rec)
    rec = jnp.where(lane == R_G1, g1, rec)
    rec = jnp.where(lane == R_G2, g2, rec)
    r_ref[...] = rec


def _route(logits, tm=512):
    T = logits.shape[0]
    return pl.pallas_call(
        functools.partial(_route_kernel, tm=tm),
        grid=(T // tm,),
        in_specs=[pl.BlockSpec((tm, LANES), lambda i: (i, 0))],
        out_specs=[pl.BlockSpec((tm, LANES), lambda i: (i, 0)), pl.BlockSpec((1, LANES), lambda i: (0, 0))],
        out_shape=[jax.ShapeDtypeStruct((T, LANES), F32), jax.ShapeDtypeStruct((1, LANES), F32)],
        scratch_shapes=[pltpu.VMEM((1, LANES), F32)],
        compiler_params=_cp(("arbitrary",), 32),
        name="moe_route",
    )(logits)


def _positions_kernel(r_ref, off_ref, p_ref):
    rec = r_ref[...]
    off = off_ref[...]
    lane = lax.broadcasted_iota(jnp.int32, rec.shape, 1)
    e1 = rec[:, R_E1:R_E1 + 1].astype(jnp.int32)
    e2 = rec[:, R_E2:R_E2 + 1].astype(jnp.int32)
    p1 = jnp.sum(jnp.where(lane == e1, off, 0.0), axis=-1, keepdims=True) + rec[:, R_RANK1:R_RANK1 + 1]
    p2 = jnp.sum(jnp.where(lane == e2, off, 0.0), axis=-1, keepdims=True) + rec[:, R_RANK2:R_RANK2 + 1]
    out = jnp.where(lane == 0, p1, jnp.where(lane == 1, p2, 0.0))
    p_ref[...] = out.astype(jnp.int32)


def _positions(rec, off_row, tm=512):
    T = rec.shape[0]
    return pl.pallas_call(
        _positions_kernel,
        grid=(T // tm,),
        in_specs=[pl.BlockSpec((tm, LANES), lambda i: (i, 0)), pl.BlockSpec((1, LANES), lambda i: (0, 0))],
        out_specs=pl.BlockSpec((tm, LANES), lambda i: (i, 0)),
        out_shape=jax.ShapeDtypeStruct((T, LANES), jnp.int32),
        compiler_params=_cp(("parallel",), 32),
        name="moe_positions",
    )(rec, off_row)


def _invert_kernel(pos_ref, src_ref, *, tm, n_rows):
    i = pl.program_id(0)

    @pl.when(i == 0)
    def _():
        def clear(r, c):
            src_ref[r] = 0
            return c

        lax.fori_loop(0, n_rows, clear, 0, unroll=16)

    def body(r, c):
        t = i * tm + r
        src_ref[pos_ref[2 * r]] = t
        src_ref[pos_ref[2 * r + 1]] = t
        return c

    lax.fori_loop(0, tm, body, 0, unroll=8)


def _invert(pos_flat, n_rows, tm=512):
    T = pos_flat.shape[0] // 2
    return pl.pallas_call(
        functools.partial(_invert_kernel, tm=tm, n_rows=n_rows),
        grid=(T // tm,),
        in_specs=[pl.BlockSpec((2 * tm,), lambda i: (i,), memory_space=pltpu.SMEM)],
        out_specs=pl.BlockSpec(memory_space=pltpu.SMEM),
        out_shape=jax.ShapeDtypeStruct((n_rows,), jnp.int32),
        compiler_params=_cp(("arbitrary",), 32),
        name="moe_invert",
    )(pos_flat)


def _gather_kernel(src_ref, h_ref, o_ref, sem, *, tm):
    def row_copy(r):
        return pltpu.make_async_copy(h_ref.at[pl.ds(src_ref[r], 1), :], o_ref.at[pl.ds(r, 1), :], sem)

    def start(r, c):
        row_copy(r).start()
        return c

    def wait(r, c):
        row_copy(r).wait()
        return c

    lax.fori_loop(0, tm, start, 0)
    lax.fori_loop(0, tm, wait, 0)


def _gather_rows(src, h, tm=512):
    n_rows = src.shape[0]
    return pl.pallas_call(
        functools.partial(_gather_kernel, tm=tm),
        grid=(n_rows // tm,),
        in_specs=[
            pl.BlockSpec((tm,), lambda i: (i,), memory_space=pltpu.SMEM),
            pl.BlockSpec(memory_space=pl.ANY),
        ],
        out_specs=pl.BlockSpec((tm, D_MODEL), lambda i: (i, 0)),
        out_shape=jax.ShapeDtypeStruct((n_rows, D_MODEL), h.dtype),
        scratch_shapes=[pltpu.SemaphoreType.DMA(())],
        compiler_params=_cp(("arbitrary",), 32),
        name="moe_gather",
    )(src, h)


def _expert_kernel(te_ref, na_ref, x_ref, wg_ref, wu_ref, wd_ref, o_ref, hb_ref):
    i = pl.program_id(0)
    f = pl.program_id(1)
    dot = functools.partial(jnp.dot, preferred_element_type=F32)

    @pl.when(f == 0)
    def _():
        o_ref[...] = jnp.zeros(o_ref.shape, F32)

    @pl.when(i < na_ref[0])
    def _():
        @pl.when(f == 0)
        def _():
            hb_ref[...] = x_ref[...].astype(BF16)

        h = hb_ref[...]
        g = dot(h, wg_ref[0])
        u = dot(h, wu_ref[0])
        a = (g * jax.nn.sigmoid(g) * u).astype(BF16)
        o_ref[...] += dot(a, wd_ref[0])


def _expert_ffn(tile_expert, n_active, xs, wg, wu, wd, tm=MOE_TM, tf=512):
    n_tiles = xs.shape[0] // tm
    F = wg.shape[2]
    nf = F // tf

    def row(i, f, te, na):
        return (jnp.minimum(i, na[0] - 1), 0)

    def fcol(i, f, na):
        return jnp.where(i < na[0], f, nf - 1)

    def expert(i, te, na):
        return te[jnp.minimum(i, na[0] - 1)]

    grid_spec = pltpu.PrefetchScalarGridSpec(
        num_scalar_prefetch=2,
        grid=(n_tiles, nf),
        in_specs=[
            pl.BlockSpec((tm, D_MODEL), row),
            pl.BlockSpec((1, D_MODEL, tf), lambda i, f, te, na: (expert(i, te, na), 0, fcol(i, f, na))),
            pl.BlockSpec((1, D_MODEL, tf), lambda i, f, te, na: (expert(i, te, na), 0, fcol(i, f, na))),
            pl.BlockSpec((1, tf, D_MODEL), lambda i, f, te, na: (expert(i, te, na), fcol(i, f, na), 0)),
        ],
        out_specs=pl.BlockSpec((tm, D_MODEL), lambda i, f, te, na: (i, 0)),
        scratch_shapes=[pltpu.VMEM((tm, D_MODEL), BF16)],
    )
    return pl.pallas_call(
        _expert_kernel,
        grid_spec=grid_spec,
        out_shape=jax.ShapeDtypeStruct(xs.shape, F32),
        compiler_params=_cp(("arbitrary", "arbitrary"), 48),
        name="moe_experts",
    )(tile_expert, n_active, xs, wg, wu, wd)


def _combine_kernel(pos_ref, ys_ref, x_ref, r_ref, nw_ref, o_ref, y1_ref, y2_ref, sem, *, tm, final_norm):
    def row_copy(r, s):
        dst = (y1_ref, y2_ref)[s].at[pl.ds(r, 1), :]
        src = ys_ref.at[pl.ds(pos_ref[2 * r + s], 1), :]
        return pltpu.make_async_copy(src, dst, sem)

    def start(r, c):
        row_copy(r, 0).start()
        row_copy(r, 1).start()
        return c

    def wait(r, c):
        row_copy(r, 0).wait()
        row_copy(r, 1).wait()
        return c

    lax.fori_loop(0, tm, start, 0)
    lax.fori_loop(0, tm, wait, 0)
    rec = r_ref[...]
    g1 = rec[:, R_G1:R_G1 + 1]
    g2 = rec[:, R_G2:R_G2 + 1]
    x = x_ref[...] + (g1 * y1_ref[...] + g2 * y2_ref[...])
    if final_norm:
        ms = jnp.mean(x * x, axis=-1, keepdims=True)
        x = x * lax.rsqrt(ms + EPS) * nw_ref[...]
    o_ref[...] = x


def _combine(pos_flat, ys, x, rec, nw, base, n_rows, final_norm, tm=256):
    rb = base // tm
    return pl.pallas_call(
        functools.partial(_combine_kernel, tm=tm, final_norm=final_norm),
        grid=(n_rows // tm,),
        in_specs=[
            pl.BlockSpec((2 * tm,), lambda i: (rb + i,), memory_space=pltpu.SMEM),
            pl.BlockSpec(memory_space=pl.ANY),
            pl.BlockSpec((tm, D_MODEL), lambda i: (rb + i, 0)),
            pl.BlockSpec((tm, LANES), lambda i: (rb + i, 0)),
            pl.BlockSpec((1, D_MODEL), lambda i: (0, 0)),
        ],
        out_specs=pl.BlockSpec((tm, D_MODEL), lambda i: (i, 0)),
        out_shape=jax.ShapeDtypeStruct((n_rows, D_MODEL), F32),
        scratch_shapes=[
            pltpu.VMEM((tm, D_MODEL), F32),
            pltpu.VMEM((tm, D_MODEL), F32),
            pltpu.SemaphoreType.DMA(()),
        ],
        compiler_params=_cp(("arbitrary",), 32),
        name="moe_combine",
    )(pos_flat, ys, x, rec, nw)


def _permute_w_in(w):
    starts = np.cumsum((0,) + REF_SPLITS)
    seg = {n: w[:, starts[i]:starts[i + 1]] for i, n in enumerate(REF_NAMES)}
    main = jnp.concatenate([seg[n] for n in MY_ORDER], axis=1).astype(BF16)
    cab = jnp.pad(seg["cab"], ((0, 0), (0, LANES - 32))).astype(BF16)
    return main, cab


def _lane_row(v):
    v = v.reshape(1, -1).astype(F32)
    return jnp.pad(v, ((0, 0), (0, LANES - v.shape[1])))


def _mixers(proj, cab, groups, p):
    ya, yb, yc = [], [], []
    bias = _na_bias_tables(p["rpb"])
    for base, B, S in groups:
        cos, sin = _rope_tables(S)
        qa, ka = _attn_prep(proj, cos, sin, p["q_norm"], p["k_norm"], base, B, S)
        ya.append(_flash(qa, ka, proj, base, B, S))
        yb.append(_na(proj, bias, base, B, S))
        qn, kn, vc, gates = _gdn_prep(proj, cab, p["conv_w"], p["nega"], p["dtb"], base, B, S)
        gates_t = gates.reshape(B * S // CHUNK, CHUNK, LANES).transpose(0, 2, 1)
        o_f, o_b = _gdn_scan(qn, kn, vc, gates, gates_t, B, S)
        yc.append(_gdn_out(o_f, o_b, proj, p["out_norm"], base))
    return jnp.concatenate(ya), jnp.concatenate(yb), jnp.concatenate(yc)


def _moe(h, x, logits, wg, wu, wd, final_nw, groups):
    T = h.shape[0]
    rec, cnt = _route(logits)
    counts = cnt[0, :N_EXPERTS].astype(jnp.int32)
    padded = ((counts + MOE_TM - 1) // MOE_TM) * MOE_TM
    ends = jnp.cumsum(padded)
    off = ends - padded
    n_rows = 2 * T + N_EXPERTS * MOE_TM
    n_tiles = n_rows // MOE_TM
    n_active = (ends[-1] // MOE_TM).astype(jnp.int32).reshape(1)
    tile_start = jnp.arange(n_tiles, dtype=jnp.int32) * MOE_TM
    tile_expert = jnp.minimum(jnp.sum(tile_start[:, None] >= ends[None, :], axis=1), N_EXPERTS - 1).astype(jnp.int32)
    off_row = _lane_row(off.astype(F32))
    pos = _positions(rec, off_row)[:, :2].reshape(-1)
    xs = _gather_rows(_invert(pos, n_rows), h)
    ys = _expert_ffn(tile_expert, n_active, xs, wg, wu, wd)
    outs = []
    for base, B, S in groups:
        y = _combine(pos, ys, x, rec, final_nw, base, B * S, final_norm=True)
        outs.append(y.reshape(B, S, D_MODEL))
    return outs


def kernel(x_prompt, x_sample, mix_norm, w_in, q_norm, k_norm, rpb, conv_w, a_log, dt_bias, out_norm, w_out,
           ffn_norm, dense_gate, dense_up, dense_down, router, moe_gate, moe_up, moe_down, final_norm):
    depth = mix_norm.shape[0]
    assert depth == 2, "layer 0 dense FFN, layer 1 MoE FFN followed by the final norm"
    Bp, Sp, _ = x_prompt.shape
    Bs, Ss, _ = x_sample.shape
    groups = ((0, Bp, Sp), (Bp * Sp, Bs, Ss))
    x = jnp.concatenate([x_prompt.reshape(Bp * Sp, D_MODEL), x_sample.reshape(Bs * Ss, D_MODEL)], axis=0)
    outs = None
    for l in range(depth):
        w_main, w_cab = _permute_w_in(w_in[l])
        p = {
            "q_norm": _lane_row(q_norm[l]), "k_norm": _lane_row(k_norm[l]), "rpb": rpb[l],
            "conv_w": conv_w[l].astype(F32),
            "nega": _lane_row(-jnp.exp(a_log[l].astype(F32))), "dtb": _lane_row(dt_bias[l]),
            "out_norm": _lane_row(out_norm[l]),
        }
        proj, cab = _inproj(x, mix_norm[l].reshape(1, D_MODEL), w_main, w_cab)
        ya, yb, yc = _mixers(proj, cab, groups, p)
        wo = w_out[l].astype(BF16)
        fnw = ffn_norm[l].reshape(1, D_MODEL)
        if l % 2 == 0:
            x, h = _outproj(ya, yb, yc, x, wo, fnw)
            x = _dense_ffn(h, x, dense_gate[l // 2].astype(BF16), dense_up[l // 2].astype(BF16),
                           dense_down[l // 2].astype(BF16))
        else:
            wr = jnp.pad(router[l // 2].astype(F32), ((0, 0), (0, LANES - N_EXPERTS)))
            x, h, logits = _outproj(ya, yb, yc, x, wo, fnw, jnp.stack(_split3(wr)))
            outs = _moe(h, x, logits, moe_gate[l // 2].astype(BF16), moe_up[l // 2].astype(BF16),
                        moe_down[l // 2].astype(BF16), final_norm.reshape(1, D_MODEL), groups)
    return tuple(outs)
```

```python
import functools

import numpy as np
import jax
import jax.numpy as jnp
from jax import lax
from jax.experimental import pallas as pl
from jax.experimental.pallas import tpu as pltpu

F32 = jnp.float32
BF16 = jnp.bfloat16
EPS = 1e-6

D_MODEL = 2048
GRID_W = 64
HEAD_DIM = 128
ATT_HEADS = 4
ATT_KV_HEADS = 2
ROPE_THETA = 10000.0
NA_HEADS = 4
NA_WIN_ROWS = 8
NA_WIN_COLS = 16
DN_HEADS = 8
CONV_K = 5
CHUNK = 64
N_EXPERTS = 8
LANES = 128

PROJ_MAIN = 6656
REF_SPLITS = (512, 256, 256, 512, 512, 512, 1024, 1024, 1024, 1024, 32)
REF_NAMES = ("aq", "ak", "av", "bq", "bk", "bv", "cq", "ck", "cv", "cz", "cab")
MY_ORDER = ("cq", "ck", "cv", "cz", "aq", "bq", "bk", "bv", "ak", "av")
COL = {}
_off = 0
for _n in MY_ORDER:
    COL[_n] = _off
    _off += REF_SPLITS[REF_NAMES.index(_n)]
assert _off == PROJ_MAIN

MIB = 1024 * 1024


def _cp(sem, vmem_mib):
    return pltpu.CompilerParams(dimension_semantics=sem, vmem_limit_bytes=vmem_mib * MIB)


def _inproj_kernel(x_ref, nw_ref, w_ref, wc_ref, o_ref, oc_ref, h_ref):
    @pl.when(pl.program_id(1) == 0)
    def _():
        x = x_ref[...]
        ms = jnp.mean(x * x, axis=-1, keepdims=True)
        hb = (x * lax.rsqrt(ms + EPS) * nw_ref[...]).astype(BF16)
        h_ref[...] = hb
        oc_ref[...] = jnp.dot(hb, wc_ref[...], preferred_element_type=F32)

    o_ref[...] = jnp.dot(h_ref[...], w_ref[...], preferred_element_type=F32).astype(o_ref.dtype)


def _inproj(x, nw, w, wc, tm=1024, tn=512):
    T = x.shape[0]
    return pl.pallas_call(
        _inproj_kernel,
        grid=(T // tm, PROJ_MAIN // tn),
        in_specs=[
            pl.BlockSpec((tm, D_MODEL), lambda i, j: (i, 0)),
            pl.BlockSpec((1, D_MODEL), lambda i, j: (0, 0)),
            pl.BlockSpec((D_MODEL, tn), lambda i, j: (0, j)),
            pl.BlockSpec((D_MODEL, LANES), lambda i, j: (0, 0)),
        ],
        out_specs=[
            pl.BlockSpec((tm, tn), lambda i, j: (i, j)),
            pl.BlockSpec((tm, LANES), lambda i, j: (i, 0)),
        ],
        out_shape=[jax.ShapeDtypeStruct((T, PROJ_MAIN), BF16), jax.ShapeDtypeStruct((T, LANES), F32)],
        scratch_shapes=[pltpu.VMEM((tm, D_MODEL), BF16)],
        compiler_params=_cp(("parallel", "arbitrary"), 40),
        name="inproj",
    )(x, nw, w, wc)


def _rope_tables(S):
    half = HEAD_DIM // 2
    quarter = half // 2
    t = jnp.arange(S)
    inv = ROPE_THETA ** (-jnp.arange(quarter, dtype=F32) / quarter)
    ang_r = (t // GRID_W).astype(F32)[:, None] * inv
    ang_c = (t % GRID_W).astype(F32)[:, None] * inv
    cos = jnp.concatenate([jnp.cos(ang_r), jnp.cos(ang_r), jnp.cos(ang_c), jnp.cos(ang_c)], axis=-1)
    sin = jnp.concatenate([-jnp.sin(ang_r), jnp.sin(ang_r), -jnp.sin(ang_c), jnp.sin(ang_c)], axis=-1)
    return cos, sin


def _norm_rope(xh, nw, cos, sin, first_quarter):
    ms = jnp.mean(xh * xh, axis=-1, keepdims=True)
    xn = xh * lax.rsqrt(ms + EPS) * nw
    partner = jnp.where(first_quarter, pltpu.roll(xn, 96, 1), pltpu.roll(xn, 32, 1))
    return xn * cos + partner * sin


def _attn_prep_kernel(q_ref, k_ref, cos_ref, sin_ref, qn_ref, kn_ref, qo_ref, ko_ref):
    cos = cos_ref[...]
    sin = sin_ref[...]
    lane = lax.broadcasted_iota(jnp.int32, cos.shape, 1)
    first_quarter = (lane % 64) < 32
    scale = HEAD_DIM ** -0.5
    for h in range(ATT_HEADS):
        sl = slice(h * HEAD_DIM, (h + 1) * HEAD_DIM)
        y = _norm_rope(q_ref[:, sl].astype(F32), qn_ref[...], cos, sin, first_quarter)
        qo_ref[:, sl] = (y * scale).astype(BF16)
    for h in range(ATT_KV_HEADS):
        sl = slice(h * HEAD_DIM, (h + 1) * HEAD_DIM)
        y = _norm_rope(k_ref[:, sl].astype(F32), kn_ref[...], cos, sin, first_quarter)
        ko_ref[:, sl] = y.astype(BF16)


def _attn_prep(proj, cos, sin, qn, kn, base, B, S, tm=256):
    nb = S // tm
    rb = base // tm
    return pl.pallas_call(
        _attn_prep_kernel,
        grid=(B * nb,),
        in_specs=[
            pl.BlockSpec((tm, 512), lambda i: (rb + i, COL["aq"] // 512)),
            pl.BlockSpec((tm, 256), lambda i: (rb + i, COL["ak"] // 256)),
            pl.BlockSpec((tm, LANES), lambda i: (i % nb, 0)),
            pl.BlockSpec((tm, LANES), lambda i: (i % nb, 0)),
            pl.BlockSpec((1, LANES), lambda i: (0, 0)),
            pl.BlockSpec((1, LANES), lambda i: (0, 0)),
        ],
        out_specs=[
            pl.BlockSpec((tm, 512), lambda i: (i, 0)),
            pl.BlockSpec((tm, 256), lambda i: (i, 0)),
        ],
        out_shape=[jax.ShapeDtypeStruct((B * S, 512), BF16), jax.ShapeDtypeStruct((B * S, 256), BF16)],
        compiler_params=_cp(("parallel",), 32),
        name="attn_prep",
    )(proj, proj, cos, sin, qn, kn)


def _flash_kernel(q_ref, k_ref, v_ref, o_ref, m_ref, l_ref, acc_ref, *, tq, tk, S):
    q = q_ref[...]
    q2 = jnp.concatenate([q[:, :HEAD_DIM], q[:, HEAD_DIM:]], axis=0)
    m_ref[...] = jnp.full(m_ref.shape, -jnp.inf, F32)
    l_ref[...] = jnp.zeros(l_ref.shape, F32)
    acc_ref[...] = jnp.zeros(acc_ref.shape, F32)

    def body(j, carry):
        start = pl.multiple_of(j * tk, tk)
        k = k_ref[pl.ds(start, tk), :]
        v = v_ref[pl.ds(start, tk), :]
        s = lax.dot_general(q2, k, (((1,), (1,)), ((), ())), preferred_element_type=F32)
        blocks = [s[:, b * HEAD_DIM:(b + 1) * HEAD_DIM] for b in range(tk // HEAD_DIM)]
        m_prev = m_ref[...]
        m_new = jnp.maximum(m_prev, jnp.max(functools.reduce(jnp.maximum, blocks), axis=-1, keepdims=True))
        alpha = jnp.exp(m_prev - m_new)
        pb = [jnp.exp(b - m_new) for b in blocks]
        l_ref[...] = alpha * l_ref[...] + jnp.sum(functools.reduce(jnp.add, pb), axis=-1, keepdims=True)
        p = jnp.concatenate([b.astype(BF16) for b in pb], axis=1)
        acc_ref[...] = alpha * acc_ref[...] + jnp.dot(p, v, preferred_element_type=F32)
        m_ref[...] = m_new
        return carry

    lax.fori_loop(0, S // tk, body, 0)
    o = acc_ref[...] / l_ref[...]
    o_ref[...] = jnp.concatenate([o[:tq], o[tq:]], axis=1).astype(o_ref.dtype)


def _flash(qa, ka, proj, base, B, S, tq=512, tk=1024):
    nq = S // tq
    vb = COL["av"] // HEAD_DIM
    return pl.pallas_call(
        functools.partial(_flash_kernel, tq=tq, tk=tk, S=S),
        grid=(B, ATT_KV_HEADS, nq),
        in_specs=[
            pl.BlockSpec((tq, 256), lambda b, h, i: (b * nq + i, h)),
            pl.BlockSpec((S, HEAD_DIM), lambda b, h, i: (b, h)),
            pl.BlockSpec((S, HEAD_DIM), lambda b, h, i: (base // S + b, vb + h)),
        ],
        out_specs=pl.BlockSpec((tq, 256), lambda b, h, i: (b * nq + i, h)),
        out_shape=jax.ShapeDtypeStruct((B * S, 512), BF16),
        scratch_shapes=[
            pltpu.VMEM((2 * tq, HEAD_DIM), F32),
            pltpu.VMEM((2 * tq, HEAD_DIM), F32),
            pltpu.VMEM((2 * tq, HEAD_DIM), F32),
        ],
        compiler_params=_cp(("parallel", "parallel", "arbitrary"), 40),
        name="flash_gqa",
    )(qa, ka, proj)


NA_QROWS = 8
NA_KROWS = 16
NA_TQ = NA_QROWS * GRID_W
NA_TK = NA_KROWS * GRID_W
NA_NEG = -1e30


def _na_bias_tables(rpb):
    qr = np.arange(NA_QROWS)[:, None]
    kr = np.arange(NA_KROWS)[None, :]
    c = np.arange(GRID_W)[:, None]
    kc = np.arange(GRID_W)[None, :]
    cs = np.clip(c - NA_WIN_COLS // 2, 0, GRID_W - NA_WIN_COLS)
    col_ok = (kc >= cs) & (kc < cs + NA_WIN_COLS)
    dc = np.clip(kc - c, -(NA_WIN_COLS - 1), NA_WIN_COLS - 1) + NA_WIN_COLS - 1
    col_sel = (dc[..., None] == np.arange(2 * NA_WIN_COLS - 1)).astype(np.float32)
    by_col = jnp.einsum("hab,cmb->hacm", rpb.astype(F32), col_sel, precision=lax.Precision.HIGHEST)
    tables = []
    for off, lo in ((0, np.maximum(qr - 4, 0)), (-4, qr + 0), (-8, 8 + np.minimum(qr - 4, 0))):
        row_ok = (kr >= lo) & (kr < lo + NA_WIN_ROWS)
        dr = np.clip(off + kr - qr + NA_WIN_ROWS - 1, 0, 2 * NA_WIN_ROWS - 2)
        row_sel = (dr[..., None] == np.arange(2 * NA_WIN_ROWS - 1)).astype(np.float32)
        bias = jnp.einsum("qka,hacm->hqckm", row_sel, by_col, precision=lax.Precision.HIGHEST)
        ok = row_ok[:, None, :, None] & col_ok[None, :, None, :]
        tables.append(jnp.where(jnp.asarray(ok)[None], bias, NA_NEG).reshape(NA_HEADS, NA_TQ, NA_TK))
    return jnp.stack(tables)


def _na_kernel(q_ref, k_ref, v_ref, b_ref, o_ref, *, S):
    j = pl.program_id(2)
    start = pl.multiple_of(jnp.clip(j * NA_TQ - 4 * GRID_W, 0, S - NA_TK), 4 * GRID_W)
    k = k_ref[pl.ds(start, NA_TK), :]
    v = v_ref[pl.ds(start, NA_TK), :]
    s = lax.dot_general(q_ref[...], k, (((1,), (1,)), ((), ())), preferred_element_type=F32)
    s = s * (HEAD_DIM ** -0.5) + b_ref[0, 0]
    blocks = [s[:, b * LANES:(b + 1) * LANES] for b in range(NA_TK // LANES)]
    m = jnp.broadcast_to(jnp.max(functools.reduce(jnp.maximum, blocks), axis=-1, keepdims=True), (NA_TQ, LANES))
    pb = [jnp.exp(b - m) for b in blocks]
    l = jnp.sum(functools.reduce(jnp.add, pb), axis=-1, keepdims=True)
    p = jnp.concatenate([b.astype(BF16) for b in pb], axis=1)
    o = jnp.dot(p, v, preferred_element_type=F32)
    o_ref[...] = (o / l).astype(o_ref.dtype)


def _na(proj, bias, base, B, S):
    assert S % NA_TQ == 0 and S >= NA_TK
    nj = S // NA_TQ
    qb, kb, vb = COL["bq"] // HEAD_DIM, COL["bk"] // HEAD_DIM, COL["bv"] // HEAD_DIM

    def case(j):
        return jnp.where(j == 0, 0, jnp.where(j == nj - 1, 2, 1))

    return pl.pallas_call(
        functools.partial(_na_kernel, S=S),
        grid=(B, NA_HEADS, nj),
        in_specs=[
            pl.BlockSpec((NA_TQ, HEAD_DIM), lambda b, h, j: (base // NA_TQ + b * nj + j, qb + h)),
            pl.BlockSpec((S, HEAD_DIM), lambda b, h, j: (base // S + b, kb + h)),
            pl.BlockSpec((S, HEAD_DIM), lambda b, h, j: (base // S + b, vb + h)),
            pl.BlockSpec((1, 1, NA_TQ, NA_TK), lambda b, h, j: (case(j), h, 0, 0)),
        ],
        out_specs=pl.BlockSpec((NA_TQ, HEAD_DIM), lambda b, h, j: (b * nj + j, h)),
        out_shape=jax.ShapeDtypeStruct((B * S, NA_HEADS * HEAD_DIM), BF16),
        compiler_params=_cp(("parallel", "parallel", "arbitrary"), 40),
        name="nbr_attn",
    )(proj, proj, proj, bias)


G_GC, G_BETA, G_EGC, G_EKD, G_ELAST, G_BEXP = 0, 16, 32, 48, 64, 80
GDN_TB = 256
GDN_SUB = 4
HALO = 16


def _split3(x):
    a = x.astype(BF16)
    r = x - a.astype(F32)
    b = r.astype(BF16)
    c = (r - b.astype(F32)).astype(BF16)
    return a, b, c


def _dot_exact_lhs(m_bf16, x):
    a, b, c = _split3(x)
    d = functools.partial(jnp.dot, preferred_element_type=F32)
    return d(m_bf16, a) + d(m_bf16, b) + d(m_bf16, c)


def _gdn_prep_kernel(x_ref, xp_ref, xn_ref, cw_ref, cab_ref, nega_ref, dtb_ref,
                     q_ref, k_ref, v_ref, g_ref, ext_ref, *, nb):
    i = pl.program_id(0)
    tb = GDN_TB
    first = (i % nb) == 0
    last = (i % nb) == nb - 1
    ext_ref[0:HALO, :] = jnp.where(first, 0.0, xp_ref[...].astype(F32))
    ext_ref[HALO:HALO + tb, :] = x_ref[...].astype(F32)
    ext_ref[HALO + tb:, :] = jnp.where(last, 0.0, xn_ref[...].astype(F32))
    outs = (q_ref, k_ref, v_ref)
    for c in range(3 * DN_HEADS):
        sl = slice(c * LANES, (c + 1) * LANES)
        acc = jnp.zeros((tb, LANES), F32)
        for t in range(CONV_K):
            acc = acc + cw_ref[t:t + 1, sl] * ext_ref[HALO - CONV_K // 2 + t:HALO - CONV_K // 2 + t + tb, sl]
        y = acc * jax.nn.sigmoid(acc)
        which, h = divmod(c, DN_HEADS)
        if which < 2:
            y = y * lax.rsqrt(jnp.sum(y * y, axis=-1, keepdims=True) + EPS)
        if which == 0:
            y = y * (HEAD_DIM ** -0.5)
        outs[which][:, h * LANES:(h + 1) * LANES] = y.astype(BF16)

    cab = cab_ref[...]
    lane = lax.broadcasted_iota(jnp.int32, cab.shape, 1)
    z = cab + dtb_ref[...]
    softplus = jnp.maximum(z, 0.0) + jnp.log1p(jnp.exp(-jnp.abs(z)))
    g = jnp.where(lane < 16, nega_ref[...] * softplus, 0.0)
    beta = jnp.where((lane >= 16) & (lane < 32), jax.nn.sigmoid(cab), 0.0)
    r = lax.broadcasted_iota(jnp.int32, (tb, tb), 0)
    cidx = lax.broadcasted_iota(jnp.int32, (tb, tb), 1)
    same = (r // CHUNK) == (cidx // CHUNK)
    allm32 = jnp.where(same, 1.0, 0.0).astype(F32)
    low = jnp.where(cidx <= r, allm32, 0.0).astype(BF16)
    upp = jnp.where(cidx >= r, allm32, 0.0).astype(BF16)
    allm = allm32.astype(BF16)
    gc = jnp.where(lane < 8, _dot_exact_lhs(low, g), _dot_exact_lhs(upp, g))
    tot = _dot_exact_lhs(allm, g)
    egc = jnp.exp(gc)
    out = gc
    out = out + beta
    out = out + pltpu.roll(jnp.where(lane < 16, egc, 0.0), G_EGC, 1)
    out = out + pltpu.roll(jnp.where(lane < 16, jnp.exp(tot - gc), 0.0), G_EKD, 1)
    out = out + pltpu.roll(jnp.where(lane < 16, jnp.exp(tot), 0.0), G_ELAST, 1)
    bexp = beta * pltpu.roll(jnp.where(lane < 16, egc, 0.0), 16, 1)
    out = out + pltpu.roll(bexp, G_BEXP - 16, 1)
    g_ref[...] = out


def _gdn_prep(proj, cab, conv_w, nega, dtb, base, B, S):
    tb = GDN_TB
    nb = S // tb
    rb = base // tb
    hb = tb // HALO
    nrows = proj.shape[0] // HALO
    width = 3 * DN_HEADS * LANES
    return pl.pallas_call(
        functools.partial(_gdn_prep_kernel, nb=nb),
        grid=(B * nb,),
        in_specs=[
            pl.BlockSpec((tb, width), lambda i: (rb + i, 0)),
            pl.BlockSpec((HALO, width), lambda i: (jnp.maximum((rb + i) * hb - 1, 0), 0)),
            pl.BlockSpec((HALO, width), lambda i: (jnp.minimum((rb + i + 1) * hb, nrows - 1), 0)),
            pl.BlockSpec((CONV_K, width), lambda i: (0, 0)),
            pl.BlockSpec((tb, LANES), lambda i: (rb + i, 0)),
            pl.BlockSpec((1, LANES), lambda i: (0, 0)),
            pl.BlockSpec((1, LANES), lambda i: (0, 0)),
        ],
        out_specs=[
            pl.BlockSpec((tb, 1024), lambda i: (i, 0)),
            pl.BlockSpec((tb, 1024), lambda i: (i, 0)),
            pl.BlockSpec((tb, 1024), lambda i: (i, 0)),
            pl.BlockSpec((tb, LANES), lambda i: (i, 0)),
        ],
        out_shape=[jax.ShapeDtypeStruct((B * S, 1024), BF16)] * 3 + [jax.ShapeDtypeStruct((B * S, LANES), F32)],
        scratch_shapes=[pltpu.VMEM((tb + 2 * HALO, width), F32)],
        compiler_params=_cp(("parallel",), 40),
        name="gdn_prep",
    )(proj, proj, proj, conv_w, cab, nega, dtb)


def _gdn_scan_kernel(qf_ref, kf_ref, vf_ref, gf_ref, gtf_ref, qb_ref, kb_ref, vb_ref, gb_ref, gtb_ref,
                     of_ref, ob_ref, s_ref):
    @pl.when(pl.program_id(1) == 0)
    def _():
        s_ref[...] = jnp.zeros(s_ref.shape, F32)

    C = CHUNK
    refs = ((qf_ref, kf_ref, vf_ref, gf_ref, gtf_ref, of_ref), (qb_ref, kb_ref, vb_ref, gb_ref, gtb_ref, ob_ref))
    chains = [(d, h, j) for j in range(GDN_SUB) for h in range(DN_HEADS) for d in (0, 1)]
    ri = lax.broadcasted_iota(jnp.int32, (C, C), 0)
    ci = lax.broadcasted_iota(jnp.int32, (C, C), 1)
    incl = (ri >= ci, ri <= ci)
    strict = (ri > ci, ri < ci)
    eye = jnp.where(ri == ci, 1.0, 0.0).astype(F32)
    dot = functools.partial(jnp.dot, preferred_element_type=F32)

    def sub(ch):
        return ch[2] if ch[0] == 0 else GDN_SUB - 1 - ch[2]

    def rows(ch):
        return slice(sub(ch) * C, (sub(ch) + 1) * C)

    def colb(field, ch, width):
        idx = field + ch[0] * DN_HEADS + ch[1]
        return jnp.broadcast_to(refs[ch[0]][3][rows(ch), idx:idx + 1], (C, width))

    def rowb(field, ch):
        idx = field + ch[0] * DN_HEADS + ch[1]
        return jnp.broadcast_to(refs[ch[0]][4][sub(ch), idx:idx + 1, :], (C, C))

    def sl(ch):
        return slice(ch[1] * LANES, (ch[1] + 1) * LANES)

    def sidx(ch):
        return ch[0] * DN_HEADS + ch[1]

    q = {ch: refs[ch[0]][0][rows(ch), sl(ch)] for ch in chains}
    k = {ch: refs[ch[0]][1][rows(ch), sl(ch)] for ch in chains}
    v = {ch: refs[ch[0]][2][rows(ch), sl(ch)] for ch in chains}
    kq = {ch: lax.dot_general(jnp.concatenate([k[ch], q[ch]], axis=0), k[ch], (((1,), (1,)), ((), ())),
                              preferred_element_type=F32) for ch in chains}
    decay = {ch: jnp.where(incl[ch[0]], jnp.exp(colb(G_GC, ch, C) - rowb(G_GC, ch)), 0.0) for ch in chains}
    n = {ch: jnp.where(strict[ch[0]], -(kq[ch][:C] * colb(G_BETA, ch, C) * decay[ch]), 0.0) for ch in chains}
    qk = {ch: (kq[ch][C:] * decay[ch]).astype(BF16) for ch in chains}
    x = {ch: eye + n[ch] for ch in chains}
    nb = {ch: n[ch].astype(BF16) for ch in chains}
    p = {ch: dot(nb[ch], nb[ch]) for ch in chains}
    steps = int(np.log2(C)) - 1
    for it in range(steps):
        pb = {ch: p[ch].astype(BF16) for ch in chains}
        if it < steps - 1:
            xp = {ch: dot(pb[ch], jnp.concatenate([x[ch].astype(BF16), pb[ch]], axis=1)) for ch in chains}
            x = {ch: x[ch] + xp[ch][:, :C] for ch in chains}
            p = {ch: xp[ch][:, C:] for ch in chains}
        else:
            x = {ch: x[ch] + dot(pb[ch], x[ch].astype(BF16)) for ch in chains}
    u = {ch: dot((x[ch] * rowb(G_BETA, ch)).astype(BF16), v[ch]) for ch in chains}
    w = {ch: dot((x[ch] * rowb(G_BEXP, ch)).astype(BF16), k[ch]).astype(BF16) for ch in chains}
    kd = {ch: (k[ch].astype(F32) * colb(G_EKD, ch, LANES)).T.astype(BF16) for ch in chains}
    for j in range(GDN_SUB):
        step = [ch for ch in chains if ch[2] == j]
        sb = {ch: s_ref[sidx(ch)].astype(BF16) for ch in step}
        qs = {ch: dot(q[ch], sb[ch]) for ch in step}
        v_new = {ch: (u[ch] - dot(w[ch], sb[ch])).astype(BF16) for ch in step}
        for ch in step:
            refs[ch[0]][5][rows(ch), sl(ch)] = colb(G_EGC, ch, LANES) * qs[ch] + dot(qk[ch], v_new[ch])
        for ch in step:
            idx = G_ELAST + ch[0] * DN_HEADS + ch[1]
            r0 = sub(ch) * C
            glast = jnp.broadcast_to(refs[ch[0]][3][r0:r0 + 1, idx:idx + 1], (LANES, LANES))
            s_ref[sidx(ch)] = s_ref[sidx(ch)] * glast + dot(kd[ch], v_new[ch])


def _gdn_scan(qn, kn, vc, gates, gates_t, B, S):
    rows = GDN_SUB * CHUNK
    n = S // rows
    fwd = lambda b, c: b * n + c
    bwd = lambda b, c: b * n + (n - 1 - c)

    def specs(idx):
        return [
            pl.BlockSpec((rows, 1024), lambda b, c: (idx(b, c), 0)),
            pl.BlockSpec((rows, 1024), lambda b, c: (idx(b, c), 0)),
            pl.BlockSpec((rows, 1024), lambda b, c: (idx(b, c), 0)),
            pl.BlockSpec((rows, LANES), lambda b, c: (idx(b, c), 0)),
            pl.BlockSpec((GDN_SUB, LANES, CHUNK), lambda b, c: (idx(b, c), 0, 0)),
        ]

    return pl.pallas_call(
        _gdn_scan_kernel,
        grid=(B, n),
        in_specs=specs(fwd) + specs(bwd),
        out_specs=[pl.BlockSpec((rows, 1024), lambda b, c: (fwd(b, c), 0)),
                   pl.BlockSpec((rows, 1024), lambda b, c: (bwd(b, c), 0))],
        out_shape=[jax.ShapeDtypeStruct((B * S, 1024), F32)] * 2,
        scratch_shapes=[pltpu.VMEM((2 * DN_HEADS, LANES, LANES), F32)],
        compiler_params=_cp(("parallel", "arbitrary"), 32),
        name="gdn_scan",
    )(qn, kn, vc, gates, gates_t, qn, kn, vc, gates, gates_t)


def _gdn_out_kernel(of_ref, ob_ref, z_ref, nw_ref, y_ref):
    for h in range(DN_HEADS):
        sl = slice(h * LANES, (h + 1) * LANES)
        o = of_ref[:, sl] + ob_ref[:, sl]
        ms = jnp.mean(o * o, axis=-1, keepdims=True)
        on = o * lax.rsqrt(ms + EPS) * nw_ref[...]
        z = z_ref[:, sl].astype(F32)
        y_ref[:, sl] = (on * (z * jax.nn.sigmoid(z))).astype(y_ref.dtype)


def _gdn_out(o_f, o_b, proj, nw, base, tb=512):
    Tg = o_f.shape[0]
    return pl.pallas_call(
        _gdn_out_kernel,
        grid=(Tg // tb,),
        in_specs=[
            pl.BlockSpec((tb, 1024), lambda i: (i, 0)),
            pl.BlockSpec((tb, 1024), lambda i: (i, 0)),
            pl.BlockSpec((tb, 1024), lambda i: (base // tb + i, COL["cz"] // 1024)),
            pl.BlockSpec((1, LANES), lambda i: (0, 0)),
        ],
        out_specs=pl.BlockSpec((tb, 1024), lambda i: (i, 0)),
        out_shape=jax.ShapeDtypeStruct((Tg, 1024), BF16),
        compiler_params=_cp(("parallel",), 32),
        name="gdn_out",
    )(o_f, o_b, proj, nw)


def _outproj_kernel(ya_ref, yb_ref, yc_ref, x_ref, w_ref, nw_ref, *rest, with_router):
    dot = functools.partial(jnp.dot, preferred_element_type=F32)
    y = dot(ya_ref[...], w_ref[0:512, :]) + dot(yb_ref[...], w_ref[512:1024, :]) + dot(yc_ref[...], w_ref[1024:, :])
    x = x_ref[...] + y
    ms = jnp.mean(x * x, axis=-1, keepdims=True)
    h = x * lax.rsqrt(ms + EPS) * nw_ref[...]
    if with_router:
        wr_ref, xo_ref, h_ref, lg_ref = rest
        ha, hb, _ = _split3(h)
        wa, wb = wr_ref[0], wr_ref[1]
        lg_ref[...] = dot(ha, wa) + (dot(ha, wb) + dot(hb, wa))
    else:
        xo_ref, h_ref = rest
    xo_ref[...] = x
    h_ref[...] = h.astype(h_ref.dtype)


def _outproj(ya, yb, yc, x, w, nw, wr3=None, tm=256):
    T = x.shape[0]
    with_router = wr3 is not None
    row = lambda width: pl.BlockSpec((tm, width), lambda i: (i, 0))
    in_specs = [row(512), row(512), row(1024), row(D_MODEL),
                pl.BlockSpec((D_MODEL, D_MODEL), lambda i: (0, 0)),
                pl.BlockSpec((1, D_MODEL), lambda i: (0, 0))]
    out_specs = [row(D_MODEL), row(D_MODEL)]
    out_shape = [jax.ShapeDtypeStruct((T, D_MODEL), F32), jax.ShapeDtypeStruct((T, D_MODEL), F32 if with_router else BF16)]
    args = [ya, yb, yc, x, w, nw]
    if with_router:
        in_specs.append(pl.BlockSpec((2, D_MODEL, LANES), lambda i: (0, 0, 0)))
        out_specs.append(row(LANES))
        out_shape.append(jax.ShapeDtypeStruct((T, LANES), F32))
        args.append(wr3)
    return pl.pallas_call(
        functools.partial(_outproj_kernel, with_router=with_router),
        grid=(T // tm,),
        in_specs=in_specs,
        out_specs=out_specs,
        out_shape=out_shape,
        compiler_params=_cp(("parallel",), 48),
        name="outproj_router" if with_router else "outproj",
    )(*args)


def _ffn_kernel(h_ref, x_ref, wg_ref, wu_ref, wd_ref, o_ref):
    dot = functools.partial(jnp.dot, preferred_element_type=F32)

    @pl.when(pl.program_id(1) == 0)
    def _():
        o_ref[...] = x_ref[...]

    h = h_ref[...]
    g = dot(h, wg_ref[...])
    u = dot(h, wu_ref[...])
    a = (g * jax.nn.sigmoid(g) * u).astype(BF16)
    o_ref[...] += dot(a, wd_ref[...])


def _dense_ffn(h, x, wg, wu, wd, tm=512, tf=512):
    T = x.shape[0]
    F = wg.shape[1]
    return pl.pallas_call(
        _ffn_kernel,
        grid=(T // tm, F // tf),
        in_specs=[
            pl.BlockSpec((tm, D_MODEL), lambda i, f: (i, 0)),
            pl.BlockSpec((tm, D_MODEL), lambda i, f: (i, 0)),
            pl.BlockSpec((D_MODEL, tf), lambda i, f: (0, f)),
            pl.BlockSpec((D_MODEL, tf), lambda i, f: (0, f)),
            pl.BlockSpec((tf, D_MODEL), lambda i, f: (f, 0)),
        ],
        out_specs=pl.BlockSpec((tm, D_MODEL), lambda i, f: (i, 0)),
        out_shape=jax.ShapeDtypeStruct((T, D_MODEL), F32),
        compiler_params=_cp(("parallel", "arbitrary"), 48),
        name="dense_ffn",
    )(h, x, wg, wu, wd)


MOE_TM = 512
R_E1, R_E2, R_RANK1, R_RANK2, R_G1, R_G2 = 0, 1, 2, 3, 4, 5


def _route_kernel(lg_ref, r_ref, cnt_ref, carry_ref, *, tm):
    @pl.when(pl.program_id(0) == 0)
    def _():
        carry_ref[...] = jnp.zeros(carry_ref.shape, F32)

    lg = lg_ref[...]
    lane = lax.broadcasted_iota(jnp.int32, lg.shape, 1).astype(F32)
    l1 = jnp.where(lane < N_EXPERTS, lg, -jnp.inf)
    m1 = jnp.max(l1, axis=-1, keepdims=True)
    i1 = jnp.min(jnp.where(l1 == m1, lane, float(LANES)), axis=-1, keepdims=True)
    l2 = jnp.where(lane == i1, -jnp.inf, l1)
    m2 = jnp.max(l2, axis=-1, keepdims=True)
    i2 = jnp.min(jnp.where(l2 == m2, lane, float(LANES)), axis=-1, keepdims=True)
    e21 = jnp.exp(m2 - m1)
    g1 = 1.0 / (1.0 + e21)
    g2 = e21 / (1.0 + e21)
    oh1 = lane == i1
    oh2 = lane == i2
    oh = jnp.where(oh1 | oh2, 1.0, 0.0).astype(F32)
    r = lax.broadcasted_iota(jnp.int32, (tm, tm), 0)
    c = lax.broadcasted_iota(jnp.int32, (tm, tm), 1)
    tri = jnp.where(c < r, 1.0, 0.0).astype(BF16)
    before = jnp.dot(tri, oh.astype(BF16), preferred_element_type=F32) + carry_ref[...]
    rank1 = jnp.sum(jnp.where(oh1, before, 0.0), axis=-1, keepdims=True)
    rank2 = jnp.sum(jnp.where(oh2, before, 0.0), axis=-1, keepdims=True)
    carry_ref[...] += jnp.sum(oh, axis=0, keepdims=True)
    cnt_ref[...] = carry_ref[...]
    rec = jnp.where(lane == R_E1, i1, 0.0)
    rec = jnp.where(lane == R_E2, i2, rec)
    rec = jnp.where(lane == R_RANK1, rank1, rec)
    rec = jnp.where(lane == R_RANK2, rank2, rec)
    rec = jnp.where(lane == R_G1, g1, rec)
    rec = jnp.where(lane == R_G2, g2, rec)
    r_ref[...] = rec


def _route(logits, tm=512):
    T = logits.shape[0]
    return pl.pallas_call(
        functools.partial(_route_kernel, tm=tm),
        grid=(T // tm,),
        in_specs=[pl.BlockSpec((tm, LANES), lambda i: (i, 0))],
        out_specs=[pl.BlockSpec((tm, LANES), lambda i: (i, 0)), pl.BlockSpec((1, LANES), lambda i: (0, 0))],
        out_shape=[jax.ShapeDtypeStruct((T, LANES), F32), jax.ShapeDtypeStruct((1, LANES), F32)],
        scratch_shapes=[pltpu.VMEM((1, LANES), F32)],
        compiler_params=_cp(("arbitrary",), 32),
        name="moe_route",
    )(logits)


def _positions_kernel(r_ref, off_ref, p_ref):
    rec = r_ref[...]
    off = off_ref[...]
    lane = lax.broadcasted_iota(jnp.int32, rec.shape, 1)
    e1 = rec[:, R_E1:R_E1 + 1].astype(jnp.int32)
    e2 = rec[:, R_E2:R_E2 + 1].astype(jnp.int32)
    p1 = jnp.sum(jnp.where(lane == e1, off, 0.0), axis=-1, keepdims=True) + rec[:, R_RANK1:R_RANK1 + 1]
    p2 = jnp.sum(jnp.where(lane == e2, off, 0.0), axis=-1, keepdims=True) + rec[:, R_RANK2:R_RANK2 + 1]
    out = jnp.where(lane == 0, p1, jnp.where(lane == 1, p2, 0.0))
    p_ref[...] = out.astype(jnp.int32)


def _positions(rec, off_row, tm=512):
    T = rec.shape[0]
    return pl.pallas_call(
        _positions_kernel,
        grid=(T // tm,),
        in_specs=[pl.BlockSpec((tm, LANES), lambda i: (i, 0)), pl.BlockSpec((1, LANES), lambda i: (0, 0))],
        out_specs=pl.BlockSpec((tm, LANES), lambda i: (i, 0)),
        out_shape=jax.ShapeDtypeStruct((T, LANES), jnp.int32),
        compiler_params=_cp(("parallel",), 32),
        name="moe_positions",
    )(rec, off_row)


def _invert_kernel(pos_ref, src_ref, *, tm, n_rows):
    i = pl.program_id(0)

    @pl.when(i == 0)
    def _():
        def clear(r, c):
            src_ref[r] = 0
            return c

        lax.fori_loop(0, n_rows, clear, 0, unroll=16)

    def body(r, c):
        t = i * tm + r
        src_ref[pos_ref[2 * r]] = t
        src_ref[pos_ref[2 * r + 1]] = t
        return c

    lax.fori_loop(0, tm, body, 0, unroll=8)


def _invert(pos_flat, n_rows, tm=512):
    T = pos_flat.shape[0] // 2
    return pl.pallas_call(
        functools.partial(_invert_kernel, tm=tm, n_rows=n_rows),
        grid=(T // tm,),
        in_specs=[pl.BlockSpec((2 * tm,), lambda i: (i,), memory_space=pltpu.SMEM)],
        out_specs=pl.BlockSpec(memory_space=pltpu.SMEM),
        out_shape=jax.ShapeDtypeStruct((n_rows,), jnp.int32),
        compiler_params=_cp(("arbitrary",), 32),
        name="moe_invert",
    )(pos_flat)


def _expert_kernel(te_ref, na_ref, src_ref, nsrc_ref, h_ref, wg_ref, wu_ref, wd_ref, o_ref, xb_ref, hb_ref, sems, *, tm):
    i = pl.program_id(0)
    f = pl.program_id(1)
    n_active = na_ref[0]
    dot = functools.partial(jnp.dot, preferred_element_type=F32)

    def row_copy(idx_ref, r, slot):
        return pltpu.make_async_copy(h_ref.at[pl.ds(idx_ref[r], 1), :], xb_ref.at[slot, pl.ds(r, 1), :], sems.at[slot])

    def start_gather(idx_ref, slot):
        def body(r, c):
            row_copy(idx_ref, r, slot).start()
            return c

        lax.fori_loop(0, tm, body, 0)

    @pl.when(f == 0)
    def _():
        o_ref[...] = jnp.zeros(o_ref.shape, F32)

    @pl.when(i < n_active)
    def _():
        slot = i % 2

        @pl.when((f == 0) & (i == 0))
        def _():
            start_gather(src_ref, 0)

        @pl.when(f == 0)
        def _():
            def wait(r, c):
                row_copy(src_ref, r, slot).wait()
                return c

            lax.fori_loop(0, tm, wait, 0)
            hb_ref[...] = xb_ref[slot].astype(BF16)

        @pl.when((f == 1) & (i + 1 < n_active))
        def _():
            start_gather(nsrc_ref, 1 - slot)

        h = hb_ref[...]
        g = dot(h, wg_ref[0])
        u = dot(h, wu_ref[0])
        a = (g * jax.nn.sigmoid(g) * u).astype(BF16)
        o_ref[...] += dot(a, wd_ref[0])


def _expert_ffn(tile_expert, n_active, src, h, wg, wu, wd, tm=MOE_TM, tf=512):
    n_tiles = src.shape[0] // tm
    F = wg.shape[2]
    nf = F // tf
    assert nf >= 2

    def fcol(i, f, na):
        return jnp.where(i < na[0], f, nf - 1)

    def expert(i, te, na):
        return te[jnp.minimum(i, na[0] - 1)]

    grid_spec = pltpu.PrefetchScalarGridSpec(
        num_scalar_prefetch=2,
        grid=(n_tiles, nf),
        in_specs=[
            pl.BlockSpec((tm,), lambda i, f, te, na: (i,), memory_space=pltpu.SMEM),
            pl.BlockSpec((tm,), lambda i, f, te, na: (jnp.minimum(i + 1, n_tiles - 1),), memory_space=pltpu.SMEM),
            pl.BlockSpec(memory_space=pl.ANY),
            pl.BlockSpec((1, D_MODEL, tf), lambda i, f, te, na: (expert(i, te, na), 0, fcol(i, f, na))),
            pl.BlockSpec((1, D_MODEL, tf), lambda i, f, te, na: (expert(i, te, na), 0, fcol(i, f, na))),
            pl.BlockSpec((1, tf, D_MODEL), lambda i, f, te, na: (expert(i, te, na), fcol(i, f, na), 0)),
        ],
        out_specs=pl.BlockSpec((tm, D_MODEL), lambda i, f, te, na: (i, 0)),
        scratch_shapes=[
            pltpu.VMEM((2, tm, D_MODEL), h.dtype),
            pltpu.VMEM((tm, D_MODEL), BF16),
            pltpu.SemaphoreType.DMA((2,)),
        ],
    )
    return pl.pallas_call(
        functools.partial(_expert_kernel, tm=tm),
        grid_spec=grid_spec,
        out_shape=jax.ShapeDtypeStruct((src.shape[0], D_MODEL), F32),
        compiler_params=_cp(("arbitrary", "arbitrary"), 48),
        name="moe_experts",
    )(tile_expert, n_active, src, src, h, wg, wu, wd)


def _combine_kernel(pos_ref, ys_ref, x_ref, r_ref, nw_ref, o_ref, y1_ref, y2_ref, sem, *, tm, final_norm):
    def row_copy(r, s):
        dst = (y1_ref, y2_ref)[s].at[pl.ds(r, 1), :]
        src = ys_ref.at[pl.ds(pos_ref[2 * r + s], 1), :]
        return pltpu.make_async_copy(src, dst, sem)

    def start(r, c):
        row_copy(r, 0).start()
        row_copy(r, 1).start()
        return c

    def wait(r, c):
        row_copy(r, 0).wait()
        row_copy(r, 1).wait()
        return c

    lax.fori_loop(0, tm, start, 0)
    lax.fori_loop(0, tm, wait, 0)
    rec = r_ref[...]
    g1 = rec[:, R_G1:R_G1 + 1]
    g2 = rec[:, R_G2:R_G2 + 1]
    x = x_ref[...] + (g1 * y1_ref[...] + g2 * y2_ref[...])
    if final_norm:
        ms = jnp.mean(x * x, axis=-1, keepdims=True)
        x = x * lax.rsqrt(ms + EPS) * nw_ref[...]
    o_ref[...] = x


def _combine(pos_flat, ys, x, rec, nw, base, n_rows, final_norm, tm=256):
    rb = base // tm
    return pl.pallas_call(
        functools.partial(_combine_kernel, tm=tm, final_norm=final_norm),
        grid=(n_rows // tm,),
        in_specs=[
            pl.BlockSpec((2 * tm,), lambda i: (rb + i,), memory_space=pltpu.SMEM),
            pl.BlockSpec(memory_space=pl.ANY),
            pl.BlockSpec((tm, D_MODEL), lambda i: (rb + i, 0)),
            pl.BlockSpec((tm, LANES), lambda i: (rb + i, 0)),
            pl.BlockSpec((1, D_MODEL), lambda i: (0, 0)),
        ],
        out_specs=pl.BlockSpec((tm, D_MODEL), lambda i: (i, 0)),
        out_shape=jax.ShapeDtypeStruct((n_rows, D_MODEL), F32),
        scratch_shapes=[
            pltpu.VMEM((tm, D_MODEL), F32),
            pltpu.VMEM((tm, D_MODEL), F32),
            pltpu.SemaphoreType.DMA(()),
        ],
        compiler_params=_cp(("arbitrary",), 32),
        name="moe_combine",
    )(pos_flat, ys, x, rec, nw)


def _permute_w_in(w):
    starts = np.cumsum((0,) + REF_SPLITS)
    seg = {n: w[:, starts[i]:starts[i + 1]] for i, n in enumerate(REF_NAMES)}
    main = jnp.concatenate([seg[n] for n in MY_ORDER], axis=1).astype(BF16)
    cab = jnp.pad(seg["cab"], ((0, 0), (0, LANES - 32))).astype(BF16)
    return main, cab


def _lane_row(v):
    v = v.reshape(1, -1).astype(F32)
    return jnp.pad(v, ((0, 0), (0, LANES - v.shape[1])))


def _mixers(proj, cab, groups, p):
    ya, yb, yc = [], [], []
    bias = _na_bias_tables(p["rpb"])
    for base, B, S in groups:
        cos, sin = _rope_tables(S)
        qa, ka = _attn_prep(proj, cos, sin, p["q_norm"], p["k_norm"], base, B, S)
        ya.append(_flash(qa, ka, proj, base, B, S))
        yb.append(_na(proj, bias, base, B, S))
        qn, kn, vc, gates = _gdn_prep(proj, cab, p["conv_w"], p["nega"], p["dtb"], base, B, S)
        gates_t = gates.reshape(B * S // CHUNK, CHUNK, LANES).transpose(0, 2, 1)
        o_f, o_b = _gdn_scan(qn, kn, vc, gates, gates_t, B, S)
        yc.append(_gdn_out(o_f, o_b, proj, p["out_norm"], base))
    return jnp.concatenate(ya), jnp.concatenate(yb), jnp.concatenate(yc)


def _moe(h, x, logits, wg, wu, wd, final_nw, groups):
    T = h.shape[0]
    rec, cnt = _route(logits)
    counts = cnt[0, :N_EXPERTS].astype(jnp.int32)
    padded = ((counts + MOE_TM - 1) // MOE_TM) * MOE_TM
    ends = jnp.cumsum(padded)
    off = ends - padded
    n_rows = 2 * T + N_EXPERTS * MOE_TM
    n_tiles = n_rows // MOE_TM
    n_active = (ends[-1] // MOE_TM).astype(jnp.int32).reshape(1)
    tile_start = jnp.arange(n_tiles, dtype=jnp.int32) * MOE_TM
    tile_expert = jnp.minimum(jnp.sum(tile_start[:, None] >= ends[None, :], axis=1), N_EXPERTS - 1).astype(jnp.int32)
    off_row = _lane_row(off.astype(F32))
    pos = _positions(rec, off_row)[:, :2].reshape(-1)
    ys = _expert_ffn(tile_expert, n_active, _invert(pos, n_rows), h, wg, wu, wd)
    outs = []
    for base, B, S in groups:
        y = _combine(pos, ys, x, rec, final_nw, base, B * S, final_norm=True)
        outs.append(y.reshape(B, S, D_MODEL))
    return outs


def kernel(x_prompt, x_sample, mix_norm, w_in, q_norm, k_norm, rpb, conv_w, a_log, dt_bias, out_norm, w_out,
           ffn_norm, dense_gate, dense_up, dense_down, router, moe_gate, moe_up, moe_down, final_norm):
    depth = mix_norm.shape[0]
    assert depth == 2, "layer 0 dense FFN, layer 1 MoE FFN followed by the final norm"
    Bp, Sp, _ = x_prompt.shape
    Bs, Ss, _ = x_sample.shape
    groups = ((0, Bp, Sp), (Bp * Sp, Bs, Ss))
    x = jnp.concatenate([x_prompt.reshape(Bp * Sp, D_MODEL), x_sample.reshape(Bs * Ss, D_MODEL)], axis=0)
    outs = None
    for l in range(depth):
        w_main, w_cab = _permute_w_in(w_in[l])
        p = {
            "q_norm": _lane_row(q_norm[l]), "k_norm": _lane_row(k_norm[l]), "rpb": rpb[l],
            "conv_w": conv_w[l].astype(F32),
            "nega": _lane_row(-jnp.exp(a_log[l].astype(F32))), "dtb": _lane_row(dt_bias[l]),
            "out_norm": _lane_row(out_norm[l]),
        }
        proj, cab = _inproj(x, mix_norm[l].reshape(1, D_MODEL), w_main, w_cab)
        ya, yb, yc = _mixers(proj, cab, groups, p)
        wo = w_out[l].astype(BF16)
        fnw = ffn_norm[l].reshape(1, D_MODEL)
        if l % 2 == 0:
            x, h = _outproj(ya, yb, yc, x, wo, fnw)
            x = _dense_ffn(h, x, dense_gate[l // 2].astype(BF16), dense_up[l // 2].astype(BF16),
                           dense_down[l // 2].astype(BF16))
        else:
            wr = jnp.pad(router[l // 2].astype(F32), ((0, 0), (0, LANES - N_EXPERTS)))
            x, h, logits = _outproj(ya, yb, yc, x, wo, fnw, jnp.stack(_split3(wr)[:2]))
            outs = _moe(h, x, logits, moe_gate[l // 2].astype(BF16), moe_up[l // 2].astype(BF16),
                        moe_down[l // 2].astype(BF16), final_norm.reshape(1, D_MODEL), groups)
    return tuple(outs)
```

```python
import functools

import numpy as np
import jax
import jax.numpy as jnp
from jax import lax
from jax.experimental import pallas as pl
from jax.experimental.pallas import tpu as pltpu

F32 = jnp.float32
BF16 = jnp.bfloat16
EPS = 1e-6

D_MODEL = 2048
GRID_W = 64
HEAD_DIM = 128
ATT_HEADS = 4
ATT_KV_HEADS = 2
ROPE_THETA = 10000.0
NA_HEADS = 4
NA_WIN_ROWS = 8
NA_WIN_COLS = 16
DN_HEADS = 8
CONV_K = 5
CHUNK = 64
N_EXPERTS = 8
LANES = 128

PROJ_MAIN = 6656
REF_SPLITS = (512, 256, 256, 512, 512, 512, 1024, 1024, 1024, 1024, 32)
REF_NAMES = ("aq", "ak", "av", "bq", "bk", "bv", "cq", "ck", "cv", "cz", "cab")
MY_ORDER = ("cq", "ck", "cv", "cz", "aq", "bq", "bk", "bv", "ak", "av")
COL = {}
_off = 0
for _n in MY_ORDER:
    COL[_n] = _off
    _off += REF_SPLITS[REF_NAMES.index(_n)]
assert _off == PROJ_MAIN

MIB = 1024 * 1024


def _cp(sem, vmem_mib):
    return pltpu.CompilerParams(dimension_semantics=sem, vmem_limit_bytes=vmem_mib * MIB)


def _inproj_kernel(x_ref, nw_ref, w_ref, wc_ref, o_ref, oc_ref, h_ref):
    @pl.when(pl.program_id(1) == 0)
    def _():
        x = x_ref[...]
        ms = jnp.mean(x * x, axis=-1, keepdims=True)
        hb = (x * lax.rsqrt(ms + EPS) * nw_ref[...]).astype(BF16)
        h_ref[...] = hb
        oc_ref[...] = jnp.dot(hb, wc_ref[...], preferred_element_type=F32)

    o_ref[...] = jnp.dot(h_ref[...], w_ref[...], preferred_element_type=F32).astype(o_ref.dtype)


def _inproj(x, nw, w, wc, tm=1024, tn=512):
    T = x.shape[0]
    return pl.pallas_call(
        _inproj_kernel,
        grid=(T // tm, PROJ_MAIN // tn),
        in_specs=[
            pl.BlockSpec((tm, D_MODEL), lambda i, j: (i, 0)),
            pl.BlockSpec((1, D_MODEL), lambda i, j: (0, 0)),
            pl.BlockSpec((D_MODEL, tn), lambda i, j: (0, j)),
            pl.BlockSpec((D_MODEL, LANES), lambda i, j: (0, 0)),
        ],
        out_specs=[
            pl.BlockSpec((tm, tn), lambda i, j: (i, j)),
            pl.BlockSpec((tm, LANES), lambda i, j: (i, 0)),
        ],
        out_shape=[jax.ShapeDtypeStruct((T, PROJ_MAIN), BF16), jax.ShapeDtypeStruct((T, LANES), F32)],
        scratch_shapes=[pltpu.VMEM((tm, D_MODEL), BF16)],
        compiler_params=_cp(("parallel", "arbitrary"), 40),
        name="inproj",
    )(x, nw, w, wc)


def _rope_tables(S):
    half = HEAD_DIM // 2
    quarter = half // 2
    t = jnp.arange(S)
    inv = ROPE_THETA ** (-jnp.arange(quarter, dtype=F32) / quarter)
    ang_r = (t // GRID_W).astype(F32)[:, None] * inv
    ang_c = (t % GRID_W).astype(F32)[:, None] * inv
    cos = jnp.concatenate([jnp.cos(ang_r), jnp.cos(ang_r), jnp.cos(ang_c), jnp.cos(ang_c)], axis=-1)
    sin = jnp.concatenate([-jnp.sin(ang_r), jnp.sin(ang_r), -jnp.sin(ang_c), jnp.sin(ang_c)], axis=-1)
    return cos, sin


def _norm_rope(xh, nw, cos, sin, first_quarter):
    ms = jnp.mean(xh * xh, axis=-1, keepdims=True)
    xn = xh * lax.rsqrt(ms + EPS) * nw
    partner = jnp.where(first_quarter, pltpu.roll(xn, 96, 1), pltpu.roll(xn, 32, 1))
    return xn * cos + partner * sin


def _attn_prep_kernel(q_ref, k_ref, cos_ref, sin_ref, qn_ref, kn_ref, qo_ref, ko_ref):
    cos = cos_ref[...]
    sin = sin_ref[...]
    lane = lax.broadcasted_iota(jnp.int32, cos.shape, 1)
    first_quarter = (lane % 64) < 32
    scale = HEAD_DIM ** -0.5
    for h in range(ATT_HEADS):
        sl = slice(h * HEAD_DIM, (h + 1) * HEAD_DIM)
        y = _norm_rope(q_ref[:, sl].astype(F32), qn_ref[...], cos, sin, first_quarter)
        qo_ref[:, sl] = (y * scale).astype(BF16)
    for h in range(ATT_KV_HEADS):
        sl = slice(h * HEAD_DIM, (h + 1) * HEAD_DIM)
        y = _norm_rope(k_ref[:, sl].astype(F32), kn_ref[...], cos, sin, first_quarter)
        ko_ref[:, sl] = y.astype(BF16)


def _attn_prep(proj, cos, sin, qn, kn, base, B, S, tm=256):
    nb = S // tm
    rb = base // tm
    return pl.pallas_call(
        _attn_prep_kernel,
        grid=(B * nb,),
        in_specs=[
            pl.BlockSpec((tm, 512), lambda i: (rb + i, COL["aq"] // 512)),
            pl.BlockSpec((tm, 256), lambda i: (rb + i, COL["ak"] // 256)),
            pl.BlockSpec((tm, LANES), lambda i: (i % nb, 0)),
            pl.BlockSpec((tm, LANES), lambda i: (i % nb, 0)),
            pl.BlockSpec((1, LANES), lambda i: (0, 0)),
            pl.BlockSpec((1, LANES), lambda i: (0, 0)),
        ],
        out_specs=[
            pl.BlockSpec((tm, 512), lambda i: (i, 0)),
            pl.BlockSpec((tm, 256), lambda i: (i, 0)),
        ],
        out_shape=[jax.ShapeDtypeStruct((B * S, 512), BF16), jax.ShapeDtypeStruct((B * S, 256), BF16)],
        compiler_params=_cp(("parallel",), 32),
        name="attn_prep",
    )(proj, proj, cos, sin, qn, kn)


def _flash_kernel(q_ref, k_ref, v_ref, o_ref, m_ref, l_ref, acc_ref, *, tq, tk, S):
    q = q_ref[...]
    q2 = jnp.concatenate([q[:, :HEAD_DIM], q[:, HEAD_DIM:]], axis=0)
    m_ref[...] = jnp.full(m_ref.shape, -jnp.inf, F32)
    l_ref[...] = jnp.zeros(l_ref.shape, F32)
    acc_ref[...] = jnp.zeros(acc_ref.shape, F32)

    def body(j, carry):
        start = pl.multiple_of(j * tk, tk)
        k = k_ref[pl.ds(start, tk), :]
        v = v_ref[pl.ds(start, tk), :]
        s = lax.dot_general(q2, k, (((1,), (1,)), ((), ())), preferred_element_type=F32)
        blocks = [s[:, b * HEAD_DIM:(b + 1) * HEAD_DIM] for b in range(tk // HEAD_DIM)]
        m_prev = m_ref[...]
        m_new = jnp.maximum(m_prev, jnp.max(functools.reduce(jnp.maximum, blocks), axis=-1, keepdims=True))
        alpha = jnp.exp(m_prev - m_new)
        pb = [jnp.exp(b - m_new) for b in blocks]
        l_ref[...] = alpha * l_ref[...] + jnp.sum(functools.reduce(jnp.add, pb), axis=-1, keepdims=True)
        p = jnp.concatenate([b.astype(BF16) for b in pb], axis=1)
        acc_ref[...] = alpha * acc_ref[...] + jnp.dot(p, v, preferred_element_type=F32)
        m_ref[...] = m_new
        return carry

    lax.fori_loop(0, S // tk, body, 0)
    o = acc_ref[...] / l_ref[...]
    o_ref[...] = jnp.concatenate([o[:tq], o[tq:]], axis=1).astype(o_ref.dtype)


def _flash(qa, ka, proj, base, B, S, tq=512, tk=1024):
    nq = S // tq
    vb = COL["av"] // HEAD_DIM
    return pl.pallas_call(
        functools.partial(_flash_kernel, tq=tq, tk=tk, S=S),
        grid=(B, ATT_KV_HEADS, nq),
        in_specs=[
            pl.BlockSpec((tq, 256), lambda b, h, i: (b * nq + i, h)),
            pl.BlockSpec((S, HEAD_DIM), lambda b, h, i: (b, h)),
            pl.BlockSpec((S, HEAD_DIM), lambda b, h, i: (base // S + b, vb + h)),
        ],
        out_specs=pl.BlockSpec((tq, 256), lambda b, h, i: (b * nq + i, h)),
        out_shape=jax.ShapeDtypeStruct((B * S, 512), BF16),
        scratch_shapes=[
            pltpu.VMEM((2 * tq, HEAD_DIM), F32),
            pltpu.VMEM((2 * tq, HEAD_DIM), F32),
            pltpu.VMEM((2 * tq, HEAD_DIM), F32),
        ],
        compiler_params=_cp(("parallel", "parallel", "arbitrary"), 40),
        name="flash_gqa",
    )(qa, ka, proj)


NA_QROWS = 8
NA_KROWS = 16
NA_TQ = NA_QROWS * GRID_W
NA_TK = NA_KROWS * GRID_W
NA_NEG = -1e30


def _na_bias_tables(rpb):
    qr = np.arange(NA_QROWS)[:, None]
    kr = np.arange(NA_KROWS)[None, :]
    c = np.arange(GRID_W)[:, None]
    kc = np.arange(GRID_W)[None, :]
    cs = np.clip(c - NA_WIN_COLS // 2, 0, GRID_W - NA_WIN_COLS)
    col_ok = (kc >= cs) & (kc < cs + NA_WIN_COLS)
    dc = np.clip(kc - c, -(NA_WIN_COLS - 1), NA_WIN_COLS - 1) + NA_WIN_COLS - 1
    col_sel = (dc[..., None] == np.arange(2 * NA_WIN_COLS - 1)).astype(np.float32)
    by_col = jnp.einsum("hab,cmb->hacm", rpb.astype(F32), col_sel, precision=lax.Precision.HIGHEST)
    tables = []
    for off, lo in ((0, np.maximum(qr - 4, 0)), (-4, qr + 0), (-8, 8 + np.minimum(qr - 4, 0))):
        row_ok = (kr >= lo) & (kr < lo + NA_WIN_ROWS)
        dr = np.clip(off + kr - qr + NA_WIN_ROWS - 1, 0, 2 * NA_WIN_ROWS - 2)
        row_sel = (dr[..., None] == np.arange(2 * NA_WIN_ROWS - 1)).astype(np.float32)
        bias = jnp.einsum("qka,hacm->hqckm", row_sel, by_col, precision=lax.Precision.HIGHEST)
        ok = row_ok[:, None, :, None] & col_ok[None, :, None, :]
        tables.append(jnp.where(jnp.asarray(ok)[None], bias, NA_NEG).reshape(NA_HEADS, NA_TQ, NA_TK))
    return jnp.stack(tables)


def _na_kernel(q_ref, k_ref, v_ref, b_ref, o_ref, *, S):
    j = pl.program_id(2)
    start = pl.multiple_of(jnp.clip(j * NA_TQ - 4 * GRID_W, 0, S - NA_TK), 4 * GRID_W)
    k = k_ref[pl.ds(start, NA_TK), :]
    v = v_ref[pl.ds(start, NA_TK), :]
    s = lax.dot_general(q_ref[...], k, (((1,), (1,)), ((), ())), preferred_element_type=F32)
    s = s * (HEAD_DIM ** -0.5) + b_ref[0, 0]
    blocks = [s[:, b * LANES:(b + 1) * LANES] for b in range(NA_TK // LANES)]
    m = jnp.broadcast_to(jnp.max(functools.reduce(jnp.maximum, blocks), axis=-1, keepdims=True), (NA_TQ, LANES))
    pb = [jnp.exp(b - m) for b in blocks]
    l = jnp.sum(functools.reduce(jnp.add, pb), axis=-1, keepdims=True)
    p = jnp.concatenate([b.astype(BF16) for b in pb], axis=1)
    o = jnp.dot(p, v, preferred_element_type=F32)
    o_ref[...] = (o / l).astype(o_ref.dtype)


def _na(proj, bias, base, B, S):
    assert S % NA_TQ == 0 and S >= NA_TK
    nj = S // NA_TQ
    qb, kb, vb = COL["bq"] // HEAD_DIM, COL["bk"] // HEAD_DIM, COL["bv"] // HEAD_DIM

    def case(j):
        return jnp.where(j == 0, 0, jnp.where(j == nj - 1, 2, 1))

    return pl.pallas_call(
        functools.partial(_na_kernel, S=S),
        grid=(B, NA_HEADS, nj),
        in_specs=[
            pl.BlockSpec((NA_TQ, HEAD_DIM), lambda b, h, j: (base // NA_TQ + b * nj + j, qb + h)),
            pl.BlockSpec((S, HEAD_DIM), lambda b, h, j: (base // S + b, kb + h)),
            pl.BlockSpec((S, HEAD_DIM), lambda b, h, j: (base // S + b, vb + h)),
            pl.BlockSpec((1, 1, NA_TQ, NA_TK), lambda b, h, j: (case(j), h, 0, 0)),
        ],
        out_specs=pl.BlockSpec((NA_TQ, HEAD_DIM), lambda b, h, j: (b * nj + j, h)),
        out_shape=jax.ShapeDtypeStruct((B * S, NA_HEADS * HEAD_DIM), BF16),
        compiler_params=_cp(("parallel", "parallel", "arbitrary"), 40),
        name="nbr_attn",
    )(proj, proj, proj, bias)


G_GC, G_BETA, G_EGC, G_EKD, G_ELAST, G_BEXP = 0, 16, 32, 48, 64, 80
GDN_TB = 256
GDN_SUB = 4
HALO = 16


def _split3(x):
    a = x.astype(BF16)
    r = x - a.astype(F32)
    b = r.astype(BF16)
    c = (r - b.astype(F32)).astype(BF16)
    return a, b, c


def _dot_exact_lhs(m_bf16, x):
    a, b, c = _split3(x)
    d = functools.partial(jnp.dot, preferred_element_type=F32)
    return d(m_bf16, a) + d(m_bf16, b) + d(m_bf16, c)


def _gdn_prep_kernel(x_ref, xp_ref, xn_ref, cw_ref, cab_ref, nega_ref, dtb_ref,
                     q_ref, k_ref, v_ref, g_ref, ext_ref, *, nb):
    i = pl.program_id(0)
    tb = GDN_TB
    first = (i % nb) == 0
    last = (i % nb) == nb - 1
    ext_ref[0:HALO, :] = jnp.where(first, 0.0, xp_ref[...].astype(F32))
    ext_ref[HALO:HALO + tb, :] = x_ref[...].astype(F32)
    ext_ref[HALO + tb:, :] = jnp.where(last, 0.0, xn_ref[...].astype(F32))
    outs = (q_ref, k_ref, v_ref)
    for c in range(3 * DN_HEADS):
        sl = slice(c * LANES, (c + 1) * LANES)
        acc = jnp.zeros((tb, LANES), F32)
        for t in range(CONV_K):
            acc = acc + cw_ref[t:t + 1, sl] * ext_ref[HALO - CONV_K // 2 + t:HALO - CONV_K // 2 + t + tb, sl]
        y = acc * jax.nn.sigmoid(acc)
        which, h = divmod(c, DN_HEADS)
        if which < 2:
            y = y * lax.rsqrt(jnp.sum(y * y, axis=-1, keepdims=True) + EPS)
        if which == 0:
            y = y * (HEAD_DIM ** -0.5)
        outs[which][:, h * LANES:(h + 1) * LANES] = y.astype(BF16)

    cab = cab_ref[...]
    lane = lax.broadcasted_iota(jnp.int32, cab.shape, 1)
    z = cab + dtb_ref[...]
    softplus = jnp.maximum(z, 0.0) + jnp.log1p(jnp.exp(-jnp.abs(z)))
    g = jnp.where(lane < 16, nega_ref[...] * softplus, 0.0)
    beta = jnp.where((lane >= 16) & (lane < 32), jax.nn.sigmoid(cab), 0.0)
    r = lax.broadcasted_iota(jnp.int32, (tb, tb), 0)
    cidx = lax.broadcasted_iota(jnp.int32, (tb, tb), 1)
    same = (r // CHUNK) == (cidx // CHUNK)
    allm32 = jnp.where(same, 1.0, 0.0).astype(F32)
    low = jnp.where(cidx <= r, allm32, 0.0).astype(BF16)
    upp = jnp.where(cidx >= r, allm32, 0.0).astype(BF16)
    allm = allm32.astype(BF16)
    gc = jnp.where(lane < 8, _dot_exact_lhs(low, g), _dot_exact_lhs(upp, g))
    tot = _dot_exact_lhs(allm, g)
    egc = jnp.exp(gc)
    out = gc
    out = out + beta
    out = out + pltpu.roll(jnp.where(lane < 16, egc, 0.0), G_EGC, 1)
    out = out + pltpu.roll(jnp.where(lane < 16, jnp.exp(tot - gc), 0.0), G_EKD, 1)
    out = out + pltpu.roll(jnp.where(lane < 16, jnp.exp(tot), 0.0), G_ELAST, 1)
    bexp = beta * pltpu.roll(jnp.where(lane < 16, egc, 0.0), 16, 1)
    out = out + pltpu.roll(bexp, G_BEXP - 16, 1)
    g_ref[...] = out


def _gdn_prep(proj, cab, conv_w, nega, dtb, base, B, S):
    tb = GDN_TB
    nb = S // tb
    rb = base // tb
    hb = tb // HALO
    nrows = proj.shape[0] // HALO
    width = 3 * DN_HEADS * LANES
    return pl.pallas_call(
        functools.partial(_gdn_prep_kernel, nb=nb),
        grid=(B * nb,),
        in_specs=[
            pl.BlockSpec((tb, width), lambda i: (rb + i, 0)),
            pl.BlockSpec((HALO, width), lambda i: (jnp.maximum((rb + i) * hb - 1, 0), 0)),
            pl.BlockSpec((HALO, width), lambda i: (jnp.minimum((rb + i + 1) * hb, nrows - 1), 0)),
            pl.BlockSpec((CONV_K, width), lambda i: (0, 0)),
            pl.BlockSpec((tb, LANES), lambda i: (rb + i, 0)),
            pl.BlockSpec((1, LANES), lambda i: (0, 0)),
            pl.BlockSpec((1, LANES), lambda i: (0, 0)),
        ],
        out_specs=[
            pl.BlockSpec((tb, 1024), lambda i: (i, 0)),
            pl.BlockSpec((tb, 1024), lambda i: (i, 0)),
            pl.BlockSpec((tb, 1024), lambda i: (i, 0)),
            pl.BlockSpec((tb, LANES), lambda i: (i, 0)),
        ],
        out_shape=[jax.ShapeDtypeStruct((B * S, 1024), BF16)] * 3 + [jax.ShapeDtypeStruct((B * S, LANES), F32)],
        scratch_shapes=[pltpu.VMEM((tb + 2 * HALO, width), F32)],
        compiler_params=_cp(("parallel",), 40),
        name="gdn_prep",
    )(proj, proj, proj, conv_w, cab, nega, dtb)


def _gdn_scan_kernel(qf_ref, kf_ref, vf_ref, gf_ref, gtf_ref, qb_ref, kb_ref, vb_ref, gb_ref, gtb_ref,
                     of_ref, ob_ref, s_ref):
    @pl.when(pl.program_id(1) == 0)
    def _():
        s_ref[...] = jnp.zeros(s_ref.shape, F32)

    C = CHUNK
    refs = ((qf_ref, kf_ref, vf_ref, gf_ref, gtf_ref, of_ref), (qb_ref, kb_ref, vb_ref, gb_ref, gtb_ref, ob_ref))
    chains = [(d, h, j) for j in range(GDN_SUB) for h in range(DN_HEADS) for d in (0, 1)]
    ri = lax.broadcasted_iota(jnp.int32, (C, C), 0)
    ci = lax.broadcasted_iota(jnp.int32, (C, C), 1)
    incl = (ri >= ci, ri <= ci)
    strict = (ri > ci, ri < ci)
    eye = jnp.where(ri == ci, 1.0, 0.0).astype(F32)
    dot = functools.partial(jnp.dot, preferred_element_type=F32)

    def sub(ch):
        return ch[2] if ch[0] == 0 else GDN_SUB - 1 - ch[2]

    def rows(ch):
        return slice(sub(ch) * C, (sub(ch) + 1) * C)

    def colb(field, ch, width):
        idx = field + ch[0] * DN_HEADS + ch[1]
        return jnp.broadcast_to(refs[ch[0]][3][rows(ch), idx:idx + 1], (C, width))

    def rowb(field, ch):
        idx = field + ch[0] * DN_HEADS + ch[1]
        return jnp.broadcast_to(refs[ch[0]][4][sub(ch), idx:idx + 1, :], (C, C))

    def sl(ch):
        return slice(ch[1] * LANES, (ch[1] + 1) * LANES)

    def sidx(ch):
        return ch[0] * DN_HEADS + ch[1]

    q = {ch: refs[ch[0]][0][rows(ch), sl(ch)] for ch in chains}
    k = {ch: refs[ch[0]][1][rows(ch), sl(ch)] for ch in chains}
    v = {ch: refs[ch[0]][2][rows(ch), sl(ch)] for ch in chains}
    kq = {ch: lax.dot_general(jnp.concatenate([k[ch], q[ch]], axis=0), k[ch], (((1,), (1,)), ((), ())),
                              preferred_element_type=F32) for ch in chains}
    decay = {ch: jnp.where(incl[ch[0]], jnp.exp(colb(G_GC, ch, C) - rowb(G_GC, ch)), 0.0) for ch in chains}
    n = {ch: jnp.where(strict[ch[0]], -(kq[ch][:C] * colb(G_BETA, ch, C) * decay[ch]), 0.0) for ch in chains}
    qk = {ch: (kq[ch][C:] * decay[ch]).astype(BF16) for ch in chains}
    x = {ch: eye + n[ch] for ch in chains}
    nb = {ch: n[ch].astype(BF16) for ch in chains}
    p = {ch: dot(nb[ch], nb[ch]) for ch in chains}
    steps = int(np.log2(C)) - 1
    for it in range(steps):
        pb = {ch: p[ch].astype(BF16) for ch in chains}
        if it < steps - 1:
            xp = {ch: dot(pb[ch], jnp.concatenate([x[ch].astype(BF16), pb[ch]], axis=1)) for ch in chains}
            x = {ch: x[ch] + xp[ch][:, :C] for ch in chains}
            p = {ch: xp[ch][:, C:] for ch in chains}
        else:
            x = {ch: x[ch] + dot(pb[ch], x[ch].astype(BF16)) for ch in chains}
    u = {ch: dot((x[ch] * rowb(G_BETA, ch)).astype(BF16), v[ch]) for ch in chains}
    w = {ch: dot((x[ch] * rowb(G_BEXP, ch)).astype(BF16), k[ch]).astype(BF16) for ch in chains}
    kd = {ch: (k[ch].astype(F32) * colb(G_EKD, ch, LANES)).T.astype(BF16) for ch in chains}
    for j in range(GDN_SUB):
        step = [ch for ch in chains if ch[2] == j]
        sb = {ch: s_ref[sidx(ch)].astype(BF16) for ch in step}
        qs = {ch: dot(q[ch], sb[ch]) for ch in step}
        v_new = {ch: (u[ch] - dot(w[ch], sb[ch])).astype(BF16) for ch in step}
        for ch in step:
            refs[ch[0]][5][rows(ch), sl(ch)] = colb(G_EGC, ch, LANES) * qs[ch] + dot(qk[ch], v_new[ch])
        for ch in step:
            idx = G_ELAST + ch[0] * DN_HEADS + ch[1]
            r0 = sub(ch) * C
            glast = jnp.broadcast_to(refs[ch[0]][3][r0:r0 + 1, idx:idx + 1], (LANES, LANES))
            s_ref[sidx(ch)] = s_ref[sidx(ch)] * glast + dot(kd[ch], v_new[ch])


def _gdn_scan(qn, kn, vc, gates, gates_t, B, S):
    rows = GDN_SUB * CHUNK
    n = S // rows
    fwd = lambda b, c: b * n + c
    bwd = lambda b, c: b * n + (n - 1 - c)

    def specs(idx):
        return [
            pl.BlockSpec((rows, 1024), lambda b, c: (idx(b, c), 0)),
            pl.BlockSpec((rows, 1024), lambda b, c: (idx(b, c), 0)),
            pl.BlockSpec((rows, 1024), lambda b, c: (idx(b, c), 0)),
            pl.BlockSpec((rows, LANES), lambda b, c: (idx(b, c), 0)),
            pl.BlockSpec((GDN_SUB, LANES, CHUNK), lambda b, c: (idx(b, c), 0, 0)),
        ]

    return pl.pallas_call(
        _gdn_scan_kernel,
        grid=(B, n),
        in_specs=specs(fwd) + specs(bwd),
        out_specs=[pl.BlockSpec((rows, 1024), lambda b, c: (fwd(b, c), 0)),
                   pl.BlockSpec((rows, 1024), lambda b, c: (bwd(b, c), 0))],
        out_shape=[jax.ShapeDtypeStruct((B * S, 1024), F32)] * 2,
        scratch_shapes=[pltpu.VMEM((2 * DN_HEADS, LANES, LANES), F32)],
        compiler_params=_cp(("parallel", "arbitrary"), 32),
        name="gdn_scan",
    )(qn, kn, vc, gates, gates_t, qn, kn, vc, gates, gates_t)


def _gdn_out_kernel(of_ref, ob_ref, z_ref, nw_ref, y_ref):
    for h in range(DN_HEADS):
        sl = slice(h * LANES, (h + 1) * LANES)
        o = of_ref[:, sl] + ob_ref[:, sl]
        ms = jnp.mean(o * o, axis=-1, keepdims=True)
        on = o * lax.rsqrt(ms + EPS) * nw_ref[...]
        z = z_ref[:, sl].astype(F32)
        y_ref[:, sl] = (on * (z * jax.nn.sigmoid(z))).astype(y_ref.dtype)


def _gdn_out(o_f, o_b, proj, nw, base, tb=512):
    Tg = o_f.shape[0]
    return pl.pallas_call(
        _gdn_out_kernel,
        grid=(Tg // tb,),
        in_specs=[
            pl.BlockSpec((tb, 1024), lambda i: (i, 0)),
            pl.BlockSpec((tb, 1024), lambda i: (i, 0)),
            pl.BlockSpec((tb, 1024), lambda i: (base // tb + i, COL["cz"] // 1024)),
            pl.BlockSpec((1, LANES), lambda i: (0, 0)),
        ],
        out_specs=pl.BlockSpec((tb, 1024), lambda i: (i, 0)),
        out_shape=jax.ShapeDtypeStruct((Tg, 1024), BF16),
        compiler_params=_cp(("parallel",), 32),
        name="gdn_out",
    )(o_f, o_b, proj, nw)


def _outproj_kernel(ya_ref, yb_ref, yc_ref, x_ref, w_ref, nw_ref, *rest, with_router):
    dot = functools.partial(jnp.dot, preferred_element_type=F32)
    y = dot(ya_ref[...], w_ref[0:512, :]) + dot(yb_ref[...], w_ref[512:1024, :]) + dot(yc_ref[...], w_ref[1024:, :])
    x = x_ref[...] + y
    ms = jnp.mean(x * x, axis=-1, keepdims=True)
    h = x * lax.rsqrt(ms + EPS) * nw_ref[...]
    if with_router:
        wr_ref, xo_ref, h_ref, lg_ref = rest
        ha, hb, _ = _split3(h)
        wa, wb = wr_ref[0], wr_ref[1]
        lg_ref[...] = dot(ha, wa) + (dot(ha, wb) + dot(hb, wa))
    else:
        xo_ref, h_ref = rest
    xo_ref[...] = x
    h_ref[...] = h.astype(h_ref.dtype)


def _outproj(ya, yb, yc, x, w, nw, wr3=None, tm=256):
    T = x.shape[0]
    with_router = wr3 is not None
    row = lambda width: pl.BlockSpec((tm, width), lambda i: (i, 0))
    in_specs = [row(512), row(512), row(1024), row(D_MODEL),
                pl.BlockSpec((D_MODEL, D_MODEL), lambda i: (0, 0)),
                pl.BlockSpec((1, D_MODEL), lambda i: (0, 0))]
    out_specs = [row(D_MODEL), row(D_MODEL)]
    out_shape = [jax.ShapeDtypeStruct((T, D_MODEL), F32), jax.ShapeDtypeStruct((T, D_MODEL), F32 if with_router else BF16)]
    args = [ya, yb, yc, x, w, nw]
    if with_router:
        in_specs.append(pl.BlockSpec((2, D_MODEL, LANES), lambda i: (0, 0, 0)))
        out_specs.append(row(LANES))
        out_shape.append(jax.ShapeDtypeStruct((T, LANES), F32))
        args.append(wr3)
    return pl.pallas_call(
        functools.partial(_outproj_kernel, with_router=with_router),
        grid=(T // tm,),
        in_specs=in_specs,
        out_specs=out_specs,
        out_shape=out_shape,
        compiler_params=_cp(("parallel",), 48),
        name="outproj_router" if with_router else "outproj",
    )(*args)


def _ffn_kernel(h_ref, x_ref, wg_ref, wu_ref, wd_ref, o_ref):
    dot = functools.partial(jnp.dot, preferred_element_type=F32)

    @pl.when(pl.program_id(1) == 0)
    def _():
        o_ref[...] = x_ref[...]

    h = h_ref[...]
    g = dot(h, wg_ref[...])
    u = dot(h, wu_ref[...])
    a = (g * jax.nn.sigmoid(g) * u).astype(BF16)
    o_ref[...] += dot(a, wd_ref[...])


def _dense_ffn(h, x, wg, wu, wd, tm=512, tf=512):
    T = x.shape[0]
    F = wg.shape[1]
    return pl.pallas_call(
        _ffn_kernel,
        grid=(T // tm, F // tf),
        in_specs=[
            pl.BlockSpec((tm, D_MODEL), lambda i, f: (i, 0)),
            pl.BlockSpec((tm, D_MODEL), lambda i, f: (i, 0)),
            pl.BlockSpec((D_MODEL, tf), lambda i, f: (0, f)),
            pl.BlockSpec((D_MODEL, tf), lambda i, f: (0, f)),
            pl.BlockSpec((tf, D_MODEL), lambda i, f: (f, 0)),
        ],
        out_specs=pl.BlockSpec((tm, D_MODEL), lambda i, f: (i, 0)),
        out_shape=jax.ShapeDtypeStruct((T, D_MODEL), F32),
        compiler_params=_cp(("parallel", "arbitrary"), 48),
        name="dense_ffn",
    )(h, x, wg, wu, wd)


MOE_TM = 512
R_E1, R_E2, R_RANK1, R_RANK2, R_G1, R_G2 = 0, 1, 2, 3, 4, 5


def _route_kernel(lg_ref, r_ref, cnt_ref, carry_ref, *, tm):
    @pl.when(pl.program_id(0) == 0)
    def _():
        carry_ref[...] = jnp.zeros(carry_ref.shape, F32)

    lg = lg_ref[...]
    lane = lax.broadcasted_iota(jnp.int32, lg.shape, 1).astype(F32)
    l1 = jnp.where(lane < N_EXPERTS, lg, -jnp.inf)
    m1 = jnp.max(l1, axis=-1, keepdims=True)
    i1 = jnp.min(jnp.where(l1 == m1, lane, float(LANES)), axis=-1, keepdims=True)
    l2 = jnp.where(lane == i1, -jnp.inf, l1)
    m2 = jnp.max(l2, axis=-1, keepdims=True)
    i2 = jnp.min(jnp.where(l2 == m2, lane, float(LANES)), axis=-1, keepdims=True)
    e21 = jnp.exp(m2 - m1)
    g1 = 1.0 / (1.0 + e21)
    g2 = e21 / (1.0 + e21)
    oh1 = lane == i1
    oh2 = lane == i2
    oh = jnp.where(oh1 | oh2, 1.0, 0.0).astype(F32)
    r = lax.broadcasted_iota(jnp.int32, (tm, tm), 0)
    c = lax.broadcasted_iota(jnp.int32, (tm, tm), 1)
    tri = jnp.where(c < r, 1.0, 0.0).astype(BF16)
    before = jnp.dot(tri, oh.astype(BF16), preferred_element_type=F32) + carry_ref[...]
    rank1 = jnp.sum(jnp.where(oh1, before, 0.0), axis=-1, keepdims=True)
    rank2 = jnp.sum(jnp.where(oh2, before, 0.0), axis=-1, keepdims=True)
    carry_ref[...] += jnp.sum(oh, axis=0, keepdims=True)
    cnt_ref[...] = carry_ref[...]
    rec = jnp.where(lane == R_E1, i1, 0.0)
    rec = jnp.where(lane == R_E2, i2, rec)
    rec = jnp.where(lane == R_RANK1, rank1, rec)
    rec = jnp.where(lane == R_RANK2, rank2, rec)
    rec = jnp.where(lane == R_G1, g1, rec)
    rec = jnp.where(lane == R_G2, g2, rec)
    r_ref[...] = rec


def _route(logits, tm=512):
    T = logits.shape[0]
    return pl.pallas_call(
        functools.partial(_route_kernel, tm=tm),
        grid=(T // tm,),
        in_specs=[pl.BlockSpec((tm, LANES), lambda i: (i, 0))],
        out_specs=[pl.BlockSpec((tm, LANES), lambda i: (i, 0)), pl.BlockSpec((1, LANES), lambda i: (0, 0))],
        out_shape=[jax.ShapeDtypeStruct((T, LANES), F32), jax.ShapeDtypeStruct((1, LANES), F32)],
        scratch_shapes=[pltpu.VMEM((1, LANES), F32)],
        compiler_params=_cp(("arbitrary",), 32),
        name="moe_route",
    )(logits)


def _positions_kernel(r_ref, off_ref, p_ref):
    rec = r_ref[...]
    off = off_ref[...]
    lane = lax.broadcasted_iota(jnp.int32, rec.shape, 1)
    e1 = rec[:, R_E1:R_E1 + 1].astype(jnp.int32)
    e2 = rec[:, R_E2:R_E2 + 1].astype(jnp.int32)
    p1 = jnp.sum(jnp.where(lane == e1, off, 0.0), axis=-1, keepdims=True) + rec[:, R_RANK1:R_RANK1 + 1]
    p2 = jnp.sum(jnp.where(lane == e2, off, 0.0), axis=-1, keepdims=True) + rec[:, R_RANK2:R_RANK2 + 1]
    out = jnp.where(lane == 0, p1, jnp.where(lane == 1, p2, 0.0))
    p_ref[...] = out.astype(jnp.int32)


def _positions(rec, off_row, tm=512):
    T = rec.shape[0]
    return pl.pallas_call(
        _positions_kernel,
        grid=(T // tm,),
        in_specs=[pl.BlockSpec((tm, LANES), lambda i: (i, 0)), pl.BlockSpec((1, LANES), lambda i: (0, 0))],
        out_specs=pl.BlockSpec((tm, LANES), lambda i: (i, 0)),
        out_shape=jax.ShapeDtypeStruct((T, LANES), jnp.int32),
        compiler_params=_cp(("parallel",), 32),
        name="moe_positions",
    )(rec, off_row)


def _invert_kernel(pos_ref, src_ref, *, tm, n_rows):
    i = pl.program_id(0)

    @pl.when(i == 0)
    def _():
        def clear(r, c):
            src_ref[r] = 0
            return c

        lax.fori_loop(0, n_rows, clear, 0, unroll=16)

    def body(r, c):
        t = i * tm + r
        src_ref[pos_ref[2 * r]] = t
        src_ref[pos_ref[2 * r + 1]] = t
        return c

    lax.fori_loop(0, tm, body, 0, unroll=8)


def _invert(pos_flat, n_rows, tm=512):
    T = pos_flat.shape[0] // 2
    return pl.pallas_call(
        functools.partial(_invert_kernel, tm=tm, n_rows=n_rows),
        grid=(T // tm,),
        in_specs=[pl.BlockSpec((2 * tm,), lambda i: (i,), memory_space=pltpu.SMEM)],
        out_specs=pl.BlockSpec(memory_space=pltpu.SMEM),
        out_shape=jax.ShapeDtypeStruct((n_rows,), jnp.int32),
        compiler_params=_cp(("arbitrary",), 32),
        name="moe_invert",
    )(pos_flat)


def _expert_kernel(te_ref, na_ref, src_ref, nsrc_ref, h_ref, wg_ref, wu_ref, wd_ref, o_ref, xb_ref, hb_ref, sems, *, tm):
    i = pl.program_id(0)
    f = pl.program_id(1)
    n_active = na_ref[0]
    dot = functools.partial(jnp.dot, preferred_element_type=F32)

    def row_copy(idx_ref, r, slot):
        return pltpu.make_async_copy(h_ref.at[pl.ds(idx_ref[r], 1), :], xb_ref.at[slot, pl.ds(r, 1), :], sems.at[slot])

    def start_gather(idx_ref, slot):
        def body(r, c):
            row_copy(idx_ref, r, slot).start()
            return c

        lax.fori_loop(0, tm, body, 0, unroll=8)

    @pl.when(f == 0)
    def _():
        o_ref[...] = jnp.zeros(o_ref.shape, F32)

    @pl.when(i < n_active)
    def _():
        slot = i % 2

        @pl.when((f == 0) & (i == 0))
        def _():
            start_gather(src_ref, 0)

        @pl.when(f == 0)
        def _():
            def wait(r, c):
                row_copy(src_ref, r, slot).wait()
                return c

            lax.fori_loop(0, tm, wait, 0, unroll=8)
            hb_ref[...] = xb_ref[slot].astype(BF16)

        @pl.when((f == 1) & (i + 1 < n_active))
        def _():
            start_gather(nsrc_ref, 1 - slot)

        h = hb_ref[...]
        g = dot(h, wg_ref[0])
        u = dot(h, wu_ref[0])
        a = (g * jax.nn.sigmoid(g) * u).astype(BF16)
        o_ref[...] += dot(a, wd_ref[0])


def _expert_ffn(tile_expert, n_active, src, h, wg, wu, wd, tm=MOE_TM, tf=512):
    n_tiles = src.shape[0] // tm
    F = wg.shape[2]
    nf = F // tf
    assert nf >= 2

    def fcol(i, f, na):
        return jnp.where(i < na[0], f, nf - 1)

    def expert(i, te, na):
        return te[jnp.minimum(i, na[0] - 1)]

    grid_spec = pltpu.PrefetchScalarGridSpec(
        num_scalar_prefetch=2,
        grid=(n_tiles, nf),
        in_specs=[
            pl.BlockSpec((tm,), lambda i, f, te, na: (i,), memory_space=pltpu.SMEM),
            pl.BlockSpec((tm,), lambda i, f, te, na: (jnp.minimum(i + 1, n_tiles - 1),), memory_space=pltpu.SMEM),
            pl.BlockSpec(memory_space=pl.ANY),
            pl.BlockSpec((1, D_MODEL, tf), lambda i, f, te, na: (expert(i, te, na), 0, fcol(i, f, na))),
            pl.BlockSpec((1, D_MODEL, tf), lambda i, f, te, na: (expert(i, te, na), 0, fcol(i, f, na))),
            pl.BlockSpec((1, tf, D_MODEL), lambda i, f, te, na: (expert(i, te, na), fcol(i, f, na), 0)),
        ],
        out_specs=pl.BlockSpec((tm, D_MODEL), lambda i, f, te, na: (i, 0)),
        scratch_shapes=[
            pltpu.VMEM((2, tm, D_MODEL), h.dtype),
            pltpu.VMEM((tm, D_MODEL), BF16),
            pltpu.SemaphoreType.DMA((2,)),
        ],
    )
    return pl.pallas_call(
        functools.partial(_expert_kernel, tm=tm),
        grid_spec=grid_spec,
        out_shape=jax.ShapeDtypeStruct((src.shape[0], D_MODEL), F32),
        compiler_params=_cp(("arbitrary", "arbitrary"), 48),
        name="moe_experts",
    )(tile_expert, n_active, src, src, h, wg, wu, wd)


def _combine_kernel(pos_ref, ys_ref, x_ref, r_ref, nw_ref, o_ref, y1_ref, y2_ref, sem, *, tm, final_norm):
    def row_copy(r, s):
        dst = (y1_ref, y2_ref)[s].at[pl.ds(r, 1), :]
        src = ys_ref.at[pl.ds(pos_ref[2 * r + s], 1), :]
        return pltpu.make_async_copy(src, dst, sem)

    def start(r, c):
        row_copy(r, 0).start()
        row_copy(r, 1).start()
        return c

    def wait(r, c):
        row_copy(r, 0).wait()
        row_copy(r, 1).wait()
        return c

    lax.fori_loop(0, tm, start, 0, unroll=4)
    lax.fori_loop(0, tm, wait, 0, unroll=4)
    rec = r_ref[...]
    g1 = rec[:, R_G1:R_G1 + 1]
    g2 = rec[:, R_G2:R_G2 + 1]
    x = x_ref[...] + (g1 * y1_ref[...] + g2 * y2_ref[...])
    if final_norm:
        ms = jnp.mean(x * x, axis=-1, keepdims=True)
        x = x * lax.rsqrt(ms + EPS) * nw_ref[...]
    o_ref[...] = x


def _combine(pos_flat, ys, x, rec, nw, base, n_rows, final_norm, tm=256):
    rb = base // tm
    return pl.pallas_call(
        functools.partial(_combine_kernel, tm=tm, final_norm=final_norm),
        grid=(n_rows // tm,),
        in_specs=[
            pl.BlockSpec((2 * tm,), lambda i: (rb + i,), memory_space=pltpu.SMEM),
            pl.BlockSpec(memory_space=pl.ANY),
            pl.BlockSpec((tm, D_MODEL), lambda i: (rb + i, 0)),
            pl.BlockSpec((tm, LANES), lambda i: (rb + i, 0)),
            pl.BlockSpec((1, D_MODEL), lambda i: (0, 0)),
        ],
        out_specs=pl.BlockSpec((tm, D_MODEL), lambda i: (i, 0)),
        out_shape=jax.ShapeDtypeStruct((n_rows, D_MODEL), F32),
        scratch_shapes=[
            pltpu.VMEM((tm, D_MODEL), F32),
            pltpu.VMEM((tm, D_MODEL), F32),
            pltpu.SemaphoreType.DMA(()),
        ],
        compiler_params=_cp(("arbitrary",), 32),
        name="moe_combine",
    )(pos_flat, ys, x, rec, nw)


def _permute_w_in(w):
    starts = np.cumsum((0,) + REF_SPLITS)
    seg = {n: w[:, starts[i]:starts[i + 1]] for i, n in enumerate(REF_NAMES)}
    main = jnp.concatenate([seg[n] for n in MY_ORDER], axis=1).astype(BF16)
    cab = jnp.pad(seg["cab"], ((0, 0), (0, LANES - 32))).astype(BF16)
    return main, cab


def _lane_row(v):
    v = v.reshape(1, -1).astype(F32)
    return jnp.pad(v, ((0, 0), (0, LANES - v.shape[1])))


def _mixers(proj, cab, groups, p):
    ya, yb, yc = [], [], []
    bias = _na_bias_tables(p["rpb"])
    for base, B, S in groups:
        cos, sin = _rope_tables(S)
        qa, ka = _attn_prep(proj, cos, sin, p["q_norm"], p["k_norm"], base, B, S)
        ya.append(_flash(qa, ka, proj, base, B, S))
        yb.append(_na(proj, bias, base, B, S))
        qn, kn, vc, gates = _gdn_prep(proj, cab, p["conv_w"], p["nega"], p["dtb"], base, B, S)
        gates_t = gates.reshape(B * S // CHUNK, CHUNK, LANES).transpose(0, 2, 1)
        o_f, o_b = _gdn_scan(qn, kn, vc, gates, gates_t, B, S)
        yc.append(_gdn_out(o_f, o_b, proj, p["out_norm"], base))
    return jnp.concatenate(ya), jnp.concatenate(yb), jnp.concatenate(yc)


def _moe(h, x, logits, wg, wu, wd, final_nw, groups):
    T = h.shape[0]
    rec, cnt = _route(logits)
    counts = cnt[0, :N_EXPERTS].astype(jnp.int32)
    padded = ((counts + MOE_TM - 1) // MOE_TM) * MOE_TM
    ends = jnp.cumsum(padded)
    off = ends - padded
    n_rows = 2 * T + N_EXPERTS * MOE_TM
    n_tiles = n_rows // MOE_TM
    n_active = (ends[-1] // MOE_TM).astype(jnp.int32).reshape(1)
    tile_start = jnp.arange(n_tiles, dtype=jnp.int32) * MOE_TM
    tile_expert = jnp.minimum(jnp.sum(tile_start[:, None] >= ends[None, :], axis=1), N_EXPERTS - 1).astype(jnp.int32)
    off_row = _lane_row(off.astype(F32))
    pos = _positions(rec, off_row)[:, :2].reshape(-1)
    ys = _expert_ffn(tile_expert, n_active, _invert(pos, n_rows), h, wg, wu, wd)
    outs = []
    for base, B, S in groups:
        y = _combine(pos, ys, x, rec, final_nw, base, B * S, final_norm=True)
        outs.append(y.reshape(B, S, D_MODEL))
    return outs


def kernel(x_prompt, x_sample, mix_norm, w_in, q_norm, k_norm, rpb, conv_w, a_log, dt_bias, out_norm, w_out,
           ffn_norm, dense_gate, dense_up, dense_down, router, moe_gate, moe_up, moe_down, final_norm):
    depth = mix_norm.shape[0]
    assert depth == 2, "layer 0 dense FFN, layer 1 MoE FFN followed by the final norm"
    Bp, Sp, _ = x_prompt.shape
    Bs, Ss, _ = x_sample.shape
    groups = ((0, Bp, Sp), (Bp * Sp, Bs, Ss))
    x = jnp.concatenate([x_prompt.reshape(Bp * Sp, D_MODEL), x_sample.reshape(Bs * Ss, D_MODEL)], axis=0)
    outs = None
    for l in range(depth):
        w_main, w_cab = _permute_w_in(w_in[l])
        p = {
            "q_norm": _lane_row(q_norm[l]), "k_norm": _lane_row(k_norm[l]), "rpb": rpb[l],
            "conv_w": conv_w[l].astype(F32),
            "nega": _lane_row(-jnp.exp(a_log[l].astype(F32))), "dtb": _lane_row(dt_bias[l]),
            "out_norm": _lane_row(out_norm[l]),
        }
        proj, cab = _inproj(x, mix_norm[l].reshape(1, D_MODEL), w_main, w_cab)
        ya, yb, yc = _mixers(proj, cab, groups, p)
        wo = w_out[l].astype(BF16)
        fnw = ffn_norm[l].reshape(1, D_MODEL)
        if l % 2 == 0:
            x, h = _outproj(ya, yb, yc, x, wo, fnw)
            x = _dense_ffn(h, x, dense_gate[l // 2].astype(BF16), dense_up[l // 2].astype(BF16),
                           dense_down[l // 2].astype(BF16))
        else:
            wr = jnp.pad(router[l // 2].astype(F32), ((0, 0), (0, LANES - N_EXPERTS)))
            x, h, logits = _outproj(ya, yb, yc, x, wo, fnw, jnp.stack(_split3(wr)[:2]))
            outs = _moe(h, x, logits, moe_gate[l // 2].astype(BF16), moe_up[l // 2].astype(BF16),
                        moe_down[l // 2].astype(BF16), final_norm.reshape(1, D_MODEL), groups)
    return tuple(outs)
```
